```python
import jax, jax.numpy as jnp
from jax import lax
import numpy as np

D_MODEL = 2048
BATCH = 4
SEQ = 4096
DEPTH = 1
DEC_BATCH = 8
DEC_SEQ = 16
PAST_LEN = 4096

CHUNK = 64
N_META = 16
CONV_WIDTH = 1024
SHORTCONV_K = 3
N_GDN_HEADS = 8
GDN_DK = 128
GDN_DV = 128
GDN_QK = N_GDN_HEADS * GDN_DK
GDN_V = N_GDN_HEADS * GDN_DV
GDN_CONV_K = 4
QKV_WIDTH = 2 * GDN_QK + GDN_V
MIX_WIDTH = CONV_WIDTH + GDN_V
D_FF = 5632
FFN_CONV_K = 3
EPS = 1e-6
SPLITS = list(np.cumsum([CONV_WIDTH, CONV_WIDTH, CONV_WIDTH, QKV_WIDTH, GDN_V, N_GDN_HEADS]))
IN_COLS = 3 * CONV_WIDTH + QKV_WIDTH + GDN_V + 2 * N_GDN_HEADS

kernel_name = "hymba_conv_gdn_streaming_step"


def rms_norm(x, g):
    xf = x.astype(jnp.float32)
    y = xf * lax.rsqrt(jnp.mean(xf * xf, -1, keepdims=True) + EPS)
    return y * g.astype(jnp.float32)


def l2norm(x):
    return x * lax.rsqrt(jnp.sum(x * x, -1, keepdims=True) + EPS)


def causal_dwconv(x, hist, w):
    K = w.shape[0]
    T = x.shape[1]
    xp = jnp.concatenate([hist.astype(x.dtype), x], axis=1)
    y = sum(xp[:, j:j + T] * w[j] for j in range(K))
    return y, xp[:, -(K - 1):]


def gated_delta_chunked(q, k, v, g, beta, S0):
    Bn, T, H, dk = q.shape
    dv = v.shape[-1]
    n = T // CHUNK

    def to_chunks(a):
        return a.reshape(Bn, n, CHUNK, H, *a.shape[3:]).swapaxes(2, 3)

    qc, kc, vc, gc, bc = map(to_chunks, (q, k, v, g, beta))
    gcum = jnp.cumsum(gc, axis=-1)
    idx = jnp.arange(CHUNK)
    lower_incl = idx[:, None] >= idx[None, :]
    strict = idx[:, None] > idx[None, :]
    diff = gcum[..., :, None] - gcum[..., None, :]
    decay_mat = jnp.exp(jnp.where(lower_incl, diff, -jnp.inf))
    kb = kc * bc[..., None]
    M = jnp.einsum('bnhid,bnhjd->bnhij', kb, kc) * jnp.where(strict, decay_mat, 0.0)
    A = jnp.eye(CHUNK, dtype=jnp.float32) + M
    rhs = jnp.concatenate([vc * bc[..., None], kb * jnp.exp(gcum)[..., None]], axis=-1)
    sol = lax.linalg.triangular_solve(A, rhs, left_side=True, lower=True, unit_diagonal=True)
    u, w = sol[..., :dv], sol[..., dv:]
    attn_in = jnp.einsum('bnhid,bnhjd->bnhij', qc, kc) * decay_mat

    def step(S, xs):
        qn, kn, un, wn, gn, an = xs
        v_new = un - jnp.einsum('bhld,bhde->bhle', wn, S)
        o = (jnp.einsum('bhld,bhde->bhle', qn * jnp.exp(gn)[..., None], S)
             + jnp.einsum('bhij,bhje->bhie', an, v_new))
        g_last = gn[..., -1:]
        S = (S * jnp.exp(g_last)[..., None]
             + jnp.einsum('bhld,bhle->bhde', kn * jnp.exp(g_last - gn)[..., None], v_new))
        return S, o

    xs = tuple(jnp.moveaxis(a, 1, 0) for a in (qc, kc, u, w, gcum, attn_in))
    S, o = lax.scan(step, S0, xs)
    o = jnp.moveaxis(o, 0, 1).swapaxes(2, 3).reshape(Bn, T, H, dv)
    return o, S


def gdn_run(q, k, v, g, beta, S0, front):
    T = q.shape[1]
    back = (-(T + front)) % CHUNK

    def pad(a):
        return jnp.pad(a, [(0, 0), (front, back)] + [(0, 0)] * (a.ndim - 2))

    o, S = gated_delta_chunked(pad(q), pad(k), pad(v), pad(g), pad(beta), S0)
    return o[:, front:front + T], S


def layer_forward(x, hist_a, hist_qkv, S0, hist_ffn, front,
                  g_pre_mix, w_in, w_conv_a, g_norm_a, w_conv_gdn, a_log, dt_bias,
                  g_norm_gdn, w_out, g_post_mix, g_pre_ffn, w_up, w_conv_ffn, w_down, g_post_ffn):
    Bn, T, _ = x.shape
    dt = x.dtype
    h = rms_norm(x, g_pre_mix).astype(dt)
    proj = h @ w_in
    a_h, a_c, a_b, qkv, z, b_logit, a_logit = jnp.split(proj, SPLITS, axis=-1)
    conv_a, new_hist_a = causal_dwconv(a_c * a_h, hist_a, w_conv_a)
    y_a = rms_norm(a_b * conv_a, g_norm_a)
    qkv_c, new_hist_qkv = causal_dwconv(qkv, hist_qkv, w_conv_gdn)
    qkv_c = jax.nn.silu(qkv_c.astype(jnp.float32))
    q, k, v = jnp.split(qkv_c, [GDN_QK, 2 * GDN_QK], axis=-1)
    q = l2norm(q.reshape(Bn, T, N_GDN_HEADS, GDN_DK)) * (GDN_DK ** -0.5)
    k = l2norm(k.reshape(Bn, T, N_GDN_HEADS, GDN_DK))
    v = v.reshape(Bn, T, N_GDN_HEADS, GDN_DV)
    beta = jax.nn.sigmoid(b_logit.astype(jnp.float32))
    g = -jnp.exp(a_log.astype(jnp.float32)) * jax.nn.softplus(
        a_logit.astype(jnp.float32) + dt_bias.astype(jnp.float32))
    o, S = gdn_run(q, k, v, g, beta, S0.astype(jnp.float32), front)
    zf = z.astype(jnp.float32).reshape(Bn, T, N_GDN_HEADS, GDN_DV)
    y_b = (rms_norm(o, g_norm_gdn) * jax.nn.silu(zf)).reshape(Bn, T, GDN_V)
    mix = jnp.concatenate([y_a, y_b], axis=-1).astype(dt) @ w_out
    x = x + rms_norm(mix, g_post_mix).astype(dt)
    h2 = rms_norm(x, g_pre_ffn).astype(dt)
    up_g, up_v = jnp.split(h2 @ w_up, [D_FF], axis=-1)
    up_gc, new_hist_ffn = causal_dwconv(up_g, hist_ffn, w_conv_ffn)
    f = (jax.nn.silu(up_gc) * up_v) @ w_down
    x = x + rms_norm(f, g_post_ffn).astype(dt)
    return x, (new_hist_a, new_hist_qkv, S.astype(dt), new_hist_ffn)


def run_stack(x, states, front, weights):
    new = []
    for l in range(DEPTH):
        x, st = layer_forward(x, *states[l], front, *(w[l] for w in weights))
        new.append(st)
    stacked = tuple(jnp.stack([s[i] for s in new]) for i in range(4))
    return x, stacked


def setup_inputs(seed: int = 0) -> dict:
    key = jax.random.key(seed)
    ks = jax.random.split(key, 24)
    f32 = jnp.float32
    nrm = lambda k, s, sc: jax.random.normal(k, s, f32) * sc
    gain = lambda k, n: 1.0 + 0.05 * jax.random.normal(k, (DEPTH, n), f32)
    dt0 = jnp.exp(jax.random.uniform(ks[20], (DEPTH, N_GDN_HEADS), f32, np.log(1e-3), np.log(1e-1)))
    return {
        "x_prompt": nrm(ks[0], (BATCH, SEQ, D_MODEL), 1.0),
        "x_sample": nrm(ks[1], (DEC_BATCH, DEC_SEQ, D_MODEL), 1.0),
        "state_conv_a": nrm(ks[2], (DEPTH, DEC_BATCH, SHORTCONV_K - 1, CONV_WIDTH), 1.0),
        "state_gdn_conv": nrm(ks[3], (DEPTH, DEC_BATCH, GDN_CONV_K - 1, QKV_WIDTH), 1.0),
        "state_gdn": nrm(ks[4], (DEPTH, DEC_BATCH, N_GDN_HEADS, GDN_DK, GDN_DV), GDN_DK ** -0.5),
        "state_ffn_conv": nrm(ks[5], (DEPTH, DEC_BATCH, FFN_CONV_K - 1, D_FF), 1.0),
        "meta_tokens": nrm(ks[6], (N_META, D_MODEL), 1.0),
        "g_pre_mix": gain(ks[7], D_MODEL),
        "w_in": nrm(ks[8], (DEPTH, D_MODEL, IN_COLS), D_MODEL ** -0.5),
        "w_conv_a": nrm(ks[9], (DEPTH, SHORTCONV_K, CONV_WIDTH), SHORTCONV_K ** -0.5),
        "g_norm_a": gain(ks[10], CONV_WIDTH),
        "w_conv_gdn": nrm(ks[11], (DEPTH, GDN_CONV_K, QKV_WIDTH), GDN_CONV_K ** -0.5),
        "a_log": jnp.log(jax.random.uniform(ks[12], (DEPTH, N_GDN_HEADS), f32, 1.0, 16.0)),
        "dt_bias": jnp.log(jnp.expm1(dt0)),
        "g_norm_gdn": gain(ks[13], GDN_DV),
        "w_out": nrm(ks[14], (DEPTH, MIX_WIDTH, D_MODEL), MIX_WIDTH ** -0.5),
        "g_post_mix": gain(ks[15], D_MODEL),
        "g_pre_ffn": gain(ks[16], D_MODEL),
        "w_up": nrm(ks[17], (DEPTH, D_MODEL, 2 * D_FF), D_MODEL ** -0.5),
        "w_conv_ffn": nrm(ks[18], (DEPTH, FFN_CONV_K, D_FF), FFN_CONV_K ** -0.5),
        "w_down": nrm(ks[19], (DEPTH, D_FF, D_MODEL), D_FF ** -0.5),
        "g_post_ffn": gain(ks[21], D_MODEL),
    }


def reference(x_prompt, x_sample, state_conv_a, state_gdn_conv, state_gdn, state_ffn_conv,
              meta_tokens, g_pre_mix, w_in, w_conv_a, g_norm_a, w_conv_gdn, a_log, dt_bias,
              g_norm_gdn, w_out, g_post_mix, g_pre_ffn, w_up, w_conv_ffn, w_down, g_post_ffn):
    weights = (g_pre_mix, w_in, w_conv_a, g_norm_a, w_conv_gdn, a_log, dt_bias, g_norm_gdn,
               w_out, g_post_mix, g_pre_ffn, w_up, w_conv_ffn, w_down, g_post_ffn)
    dt = x_prompt.dtype
    meta = jnp.broadcast_to(meta_tokens.astype(dt)[None], (BATCH, N_META, D_MODEL))
    xp = jnp.concatenate([meta, x_prompt], axis=1)
    zero_states = [(jnp.zeros((BATCH, SHORTCONV_K - 1, CONV_WIDTH), dt),
                    jnp.zeros((BATCH, GDN_CONV_K - 1, QKV_WIDTH), dt),
                    jnp.zeros((BATCH, N_GDN_HEADS, GDN_DK, GDN_DV), jnp.float32),
                    jnp.zeros((BATCH, FFN_CONV_K - 1, D_FF), dt)) for _ in range(DEPTH)]
    yp, (nca_p, ngc_p, ngd_p, nfc_p) = run_stack(xp, zero_states, CHUNK - N_META, weights)
    y_prompt = yp[:, N_META:]
    samp_states = [(state_conv_a[l], state_gdn_conv[l], state_gdn[l], state_ffn_conv[l])
                   for l in range(DEPTH)]
    y_sample, (nca_s, ngc_s, ngd_s, nfc_s) = run_stack(x_sample, samp_states, 0, weights)
    return (y_prompt, y_sample, nca_p, ngc_p, ngd_p, nfc_p, nca_s, ngc_s, ngd_s, nfc_s)
```

```python
import functools
from typing import NamedTuple

import jax
import jax.numpy as jnp
from jax import lax
from jax.experimental import pallas as pl
from jax.experimental.pallas import tpu as pltpu

F32 = jnp.float32
BF16 = jnp.bfloat16

D_MODEL = 2048
CHUNK = 64
N_META = 16
FRONT = CHUNK - N_META
CONV_WIDTH = 1024
N_HEADS = 8
HEAD_DIM = 128
GDN_QK = N_HEADS * HEAD_DIM
QKV_WIDTH = 3 * GDN_QK
D_FF = 5632
EPS = 1e-6

LANES = 128
SUBLANES = 8
BF16_ROWS = 16
VMEM_LIMIT_BYTES = 56 * 1024 * 1024

COL_TILE = 1024
N_COL_TILES = 7
P_TILES = 5
FF_TILE = 512
N_FF_TILES = D_FF // FF_TILE
MAX_TILE_ROWS = 1024
ROW_BLOCK = 32


class Rows(NamedTuple):
    nbatch: int
    tp: int
    tm: int
    nseg: int

    @property
    def tiles_per_stream(self):
        return self.tp // self.tm if self.nseg == 1 else 1

    @property
    def ntiles(self):
        return self.nbatch * self.tp // self.tm

    @property
    def nrows(self):
        return self.nbatch * self.tp


def _pick_rows(nbatch, tp):
    if tp * nbatch <= MAX_TILE_ROWS:
        return Rows(nbatch, tp, nbatch * tp, nbatch)
    tm = max(t for t in range(ROW_BLOCK, min(tp, MAX_TILE_ROWS) + 1, ROW_BLOCK) if tp % t == 0)
    return Rows(nbatch, tp, tm, 1)


def _stream_block(rows, i):
    return i // rows.tiles_per_stream if rows.nseg == 1 else i


def _is_first_tile(rows, i):
    return (i % rows.tiles_per_stream) == 0 if rows.nseg == 1 else None


def _noop_rows(rows, i, shape, row0=0, enable=True):
    r = lax.broadcasted_iota(jnp.int32, shape, 0) + row0
    if rows.nseg == 1:
        enable = jnp.logical_and(enable, _is_first_tile(rows, i))
    else:
        r = r % rows.tp
    return r < jnp.where(enable, FRONT, 0)


def _rms_scale(x):
    return lax.rsqrt(jnp.mean(x * x, axis=-1, keepdims=True) + EPS)


def _silu(x):
    return x * jax.nn.sigmoid(x)


def _stage_history(cbuf, tail, hist_ref, rows, i, nhist):
    width = cbuf.shape[1]
    srow = lax.broadcasted_iota(jnp.int32, (SUBLANES, width), 0)

    def splice(s):
        blk = pl.ds(SUBLANES + s * rows.tp + FRONT - SUBLANES, SUBLANES)
        cbuf[blk, :] = jnp.where(srow >= SUBLANES - nhist, hist_ref[s], cbuf[blk, :])

    if rows.nseg == 1:
        first = _is_first_tile(rows, i)

        @pl.when(first)
        def _():
            cbuf[0:SUBLANES, :] = jnp.zeros((SUBLANES, width), F32)
            splice(0)

        @pl.when(jnp.logical_not(first))
        def _():
            cbuf[0:SUBLANES, :] = tail[...]
    else:
        cbuf[0:SUBLANES, :] = jnp.zeros((SUBLANES, width), F32)
        for s in range(rows.nseg):
            splice(s)


def _save_history(cbuf, tail, nh_ref, rows):
    if rows.nseg == 1:
        last = cbuf[pl.ds(rows.tm, SUBLANES), :]
        tail[...] = last
        nh_ref[0] = last
    else:
        for s in range(rows.nseg):
            nh_ref[s] = cbuf[pl.ds((s + 1) * rows.tp, SUBLANES), :]


def _conv_block(cbuf, r0, nrows, cols, w):
    k = w.shape[0]
    xb = cbuf[pl.ds(r0, nrows + SUBLANES), cols]
    y = xb[SUBLANES:] * w[k - 1:k]
    for s in range(1, k):
        y = y + pltpu.roll(xb, s, axis=0)[SUBLANES:] * w[k - 1 - s:k - s]
    return y


def _row_loop(nrows, step, body):
    def wrapped(b, carry):
        body(pl.multiple_of(b * step, step))
        return carry
    lax.fori_loop(0, nrows // step, wrapped, 0)


def _inproj_kernel(x_ref, gpre_ref, w_ref, wab_ref, wca_ref, gna_ref, wcg_ref, alog_ref, dtb_ref,
                   hista_ref, histq_ref,
                   p_ref, gates_ref, nha_ref, nhq_ref,
                   h_sc, ah_sc, cbuf, tail_a, tail_q, *, rows):
    i = pl.program_id(0)
    j = pl.program_id(1)
    tm = rows.tm
    tile = pl.ds(SUBLANES, tm)

    @pl.when(j == 0)
    def _():
        def norm(r0):
            x = x_ref[pl.ds(r0, BF16_ROWS), :]
            h_sc[pl.ds(r0, BF16_ROWS), :] = (x * _rms_scale(x) * gpre_ref[...]).astype(BF16)
        _row_loop(tm, BF16_ROWS, norm)

        logits = jnp.dot(h_sc[...], wab_ref[...], preferred_element_type=F32)
        lane = lax.broadcasted_iota(jnp.int32, (tm, LANES), 1)
        beta = jax.nn.sigmoid(logits)
        g = -jnp.exp(alog_ref[...]) * jax.nn.softplus(logits + dtb_ref[...])
        val = jnp.where(lane < N_HEADS, beta, g)
        val = jnp.where(lane < 2 * N_HEADS, val, 0.0)
        gates_ref[...] = jnp.where(_noop_rows(rows, i, (tm, LANES)), 0.0, val)

        ah_sc[...] = jnp.dot(h_sc[...], w_ref[...], preferred_element_type=F32)

    @pl.when(j > 0)
    def _():
        cbuf[tile, :] = jnp.dot(h_sc[...], w_ref[...], preferred_element_type=F32)

    @pl.when(j == 1)
    def _():
        def gate(r0):
            blk = pl.ds(r0 + SUBLANES, ROW_BLOCK)
            cbuf[blk, :] = cbuf[blk, :] * ah_sc[pl.ds(r0, ROW_BLOCK), :]
        _row_loop(tm, ROW_BLOCK, gate)
        _stage_history(cbuf, tail_a, hista_ref, rows, i, 2)

        def conv(r0):
            for c in range(COL_TILE // LANES):
                cols = slice(c * LANES, (c + 1) * LANES)
                ah_sc[pl.ds(r0, ROW_BLOCK), cols] = _conv_block(cbuf, r0, ROW_BLOCK, cols, wca_ref[:, cols])
        _row_loop(tm, ROW_BLOCK, conv)
        _save_history(cbuf, tail_a, nha_ref, rows)

    @pl.when(j == 2)
    def _():
        def norm(r0):
            y = cbuf[pl.ds(r0 + SUBLANES, BF16_ROWS), :] * ah_sc[pl.ds(r0, BF16_ROWS), :]
            p_ref[pl.ds(r0, BF16_ROWS), :] = (y * _rms_scale(y) * gna_ref[...]).astype(BF16)
        _row_loop(tm, BF16_ROWS, norm)

    @pl.when(jnp.logical_and(j >= 3, j <= 5))
    def _():
        part = j - 3
        _stage_history(cbuf, tail_q.at[part], histq_ref, rows, i, 3)
        qscale = jnp.where(part == 0, HEAD_DIM ** -0.5, 1.0)

        def conv(r0):
            zero_k = _noop_rows(rows, i, (ROW_BLOCK, LANES), r0, enable=part == 1)
            for c in range(COL_TILE // LANES):
                cols = slice(c * LANES, (c + 1) * LANES)
                s = _silu(_conv_block(cbuf, r0, ROW_BLOCK, cols, wcg_ref[:, cols]))
                l2 = lax.rsqrt(jnp.sum(s * s, axis=-1, keepdims=True) + EPS) * qscale
                s = s * jnp.where(part == 2, 1.0, l2)
                p_ref[pl.ds(r0, ROW_BLOCK), cols] = jnp.where(zero_k, 0.0, s).astype(BF16)
        _row_loop(tm, ROW_BLOCK, conv)
        _save_history(cbuf, tail_q.at[part], nhq_ref, rows)

    @pl.when(j == 6)
    def _():
        def gate(r0):
            p_ref[pl.ds(r0, ROW_BLOCK), :] = _silu(cbuf[pl.ds(r0 + SUBLANES, ROW_BLOCK), :]).astype(BF16)
        _row_loop(tm, ROW_BLOCK, gate)


def _inproj(rows, x, gpre, w_main, w_ab, wca, gna, wcg, alog, dtb, hist_a, hist_q):
    tm, nseg = rows.tm, rows.nseg
    sb = functools.partial(_stream_block, rows)
    qpart = lambda j: jnp.clip(j - 3, 0, 2)
    const = lambda i, j: (0, 0)
    return pl.pallas_call(
        functools.partial(_inproj_kernel, rows=rows),
        grid=(rows.ntiles, N_COL_TILES),
        in_specs=[
            pl.BlockSpec((tm, D_MODEL), lambda i, j: (i, 0)),
            pl.BlockSpec((1, D_MODEL), const),
            pl.BlockSpec((D_MODEL, COL_TILE), lambda i, j: (0, j)),
            pl.BlockSpec((D_MODEL, LANES), const),
            pl.BlockSpec((3, CONV_WIDTH), const),
            pl.BlockSpec((1, CONV_WIDTH), const),
            pl.BlockSpec((4, COL_TILE), lambda i, j: (0, qpart(j))),
            pl.BlockSpec((1, LANES), const),
            pl.BlockSpec((1, LANES), const),
            pl.BlockSpec((nseg, SUBLANES, CONV_WIDTH), lambda i, j: (sb(i), 0, 0)),
            pl.BlockSpec((nseg, SUBLANES, COL_TILE), lambda i, j: (sb(i), 0, qpart(j))),
        ],
        out_specs=[
            pl.BlockSpec((tm, COL_TILE), lambda i, j: (i, jnp.maximum(j - 2, 0))),
            pl.BlockSpec((tm, LANES), lambda i, j: (i, 0)),
            pl.BlockSpec((nseg, SUBLANES, CONV_WIDTH), lambda i, j: (i, 0, 0)),
            pl.BlockSpec((nseg, SUBLANES, COL_TILE), lambda i, j: (i, 0, qpart(j))),
        ],
        out_shape=[
            jax.ShapeDtypeStruct((rows.nrows, P_TILES * COL_TILE), BF16),
            jax.ShapeDtypeStruct((rows.nrows, LANES), F32),
            jax.ShapeDtypeStruct((rows.ntiles * nseg, SUBLANES, CONV_WIDTH), F32),
            jax.ShapeDtypeStruct((rows.ntiles * nseg, SUBLANES, QKV_WIDTH), F32),
        ],
        scratch_shapes=[
            pltpu.VMEM((tm, D_MODEL), BF16),
            pltpu.VMEM((tm, COL_TILE), F32),
            pltpu.VMEM((tm + SUBLANES, COL_TILE), F32),
            pltpu.VMEM((SUBLANES, CONV_WIDTH), F32),
            pltpu.VMEM((3, SUBLANES, COL_TILE), F32),
        ],
        compiler_params=pltpu.CompilerParams(
            dimension_semantics=("arbitrary", "arbitrary"), vmem_limit_bytes=VMEM_LIMIT_BYTES),
        name="inproj",
    )(x, gpre, w_main, w_ab, wca, gna, wcg, alog, dtb, hist_a, hist_q)


def _dot(a, b):
    return jnp.dot(a, b, preferred_element_type=F32)


def _dot_nt(a, b):
    return lax.dot_general(a, b, (((1,), (1,)), ((), ())), preferred_element_type=F32)


def _dot_tn(a, b):
    return lax.dot_general(a, b, (((0,), (0,)), ((), ())), preferred_element_type=F32)


def _gdn_kernel(q_ref, k_ref, v_ref, zs_ref, gates_ref, s0_ref, gn_ref, yb_ref, sout_ref, s_sc):
    n = pl.program_id(1)

    @pl.when(n == 0)
    def _():
        s_sc[...] = s0_ref[0]

    gates = gates_ref[...]
    ri = lax.broadcasted_iota(jnp.int32, (CHUNK, CHUNK), 0)
    ci = lax.broadcasted_iota(jnp.int32, (CHUNK, CHUNK), 1)
    lower = ri >= ci
    strict = ri > ci
    eye = (ri == ci).astype(F32)

    tri = lower.astype(BF16)
    g_hi = gates.astype(BF16)
    rem = gates - g_hi.astype(F32)
    g_mid = rem.astype(BF16)
    g_lo = (rem - g_mid.astype(F32)).astype(BF16)
    gcum = _dot(tri, g_hi) + _dot(tri, g_mid) + _dot(tri, g_lo)
    gcum_t = gcum.T
    gates_t = gates.T
    g_last = gcum[CHUNK - 1:CHUNK, :]
    e_cum = jnp.exp(gcum)
    e_cum_t = jnp.exp(gcum_t)
    e_rest = jnp.exp(g_last - gcum)
    e_last = jnp.exp(g_last)

    for h in range(N_HEADS):
        cols = slice(h * HEAD_DIM, (h + 1) * HEAD_DIM)
        gl = N_HEADS + h
        qh = q_ref[:, cols]
        kh = k_ref[:, cols]
        vh = v_ref[:, cols]
        beta_c = gates[:, h:h + 1]
        beta_r = gates_t[h:h + 1, :]
        decay = jnp.exp(jnp.where(lower, gcum[:, gl:gl + 1] - gcum_t[gl:gl + 1, :], -jnp.inf))
        m = beta_c * _dot_nt(kh, kh) * jnp.where(strict, decay, 0.0)
        attn = _dot_nt(qh, kh) * decay

        inv = eye - m
        pw = m.astype(BF16)
        for _ in range(5):
            pw_f = _dot(pw, pw)
            pw = pw_f.astype(BF16)
            inv = inv + _dot(inv.astype(BF16), pw)

        inv_b = inv * beta_r
        u = _dot(inv_b.astype(BF16), vh)
        w = _dot((inv_b * e_cum_t[gl:gl + 1, :]).astype(BF16), kh)

        s = s_sc[h]
        s_b = s.astype(BF16)
        v_new = u - _dot(w.astype(BF16), s_b)
        o = e_cum[:, gl:gl + 1] * _dot(qh, s_b) + _dot(attn.astype(BF16), v_new.astype(BF16))
        s_sc[h] = s * e_last[:, gl:gl + 1] + _dot_tn(kh, (v_new * e_rest[:, gl:gl + 1]).astype(BF16))

        yb = o * _rms_scale(o) * gn_ref[...] * zs_ref[:, cols].astype(F32)
        yb_ref[:, cols] = yb.astype(BF16)

    @pl.when(n == pl.num_programs(1) - 1)
    def _():
        sout_ref[0] = s_sc[...]


def _gdn(rows, p, gates, s0, gn):
    nc = rows.tp // CHUNK
    blk = lambda t: pl.BlockSpec((CHUNK, COL_TILE), lambda b, n: (b * nc + n, t))
    state = pl.BlockSpec((1, N_HEADS, HEAD_DIM, HEAD_DIM), lambda b, n: (b, 0, 0, 0))
    return pl.pallas_call(
        _gdn_kernel,
        grid=(rows.nbatch, nc),
        in_specs=[blk(1), blk(2), blk(3), blk(4),
                  pl.BlockSpec((CHUNK, LANES), lambda b, n: (b * nc + n, 0)),
                  state,
                  pl.BlockSpec((1, HEAD_DIM), lambda b, n: (0, 0))],
        out_specs=[pl.BlockSpec((CHUNK, GDN_QK), lambda b, n: (b * nc + n, 0)), state],
        out_shape=[jax.ShapeDtypeStruct((rows.nrows, GDN_QK), BF16),
                   jax.ShapeDtypeStruct((rows.nbatch, N_HEADS, HEAD_DIM, HEAD_DIM), F32)],
        scratch_shapes=[pltpu.VMEM((N_HEADS, HEAD_DIM, HEAD_DIM), F32)],
        compiler_params=pltpu.CompilerParams(
            dimension_semantics=("arbitrary", "arbitrary"), vmem_limit_bytes=VMEM_LIMIT_BYTES),
        name="gdn",
    )(p, p, p, p, gates, s0, gn)


def _mixout_kernel(x_ref, ya_ref, yb_ref, w_ref, g_ref, o_ref, *, rows):
    o_ref[...] = (_dot(ya_ref[...], w_ref[0:CONV_WIDTH, :])
                  + _dot(yb_ref[...], w_ref[CONV_WIDTH:CONV_WIDTH + GDN_QK, :]))

    def norm(r0):
        blk = pl.ds(r0, BF16_ROWS)
        mix = o_ref[blk, :]
        o_ref[blk, :] = x_ref[blk, :] + mix * _rms_scale(mix) * g_ref[...]
    _row_loop(rows.tm, BF16_ROWS, norm)


def _mixout(rows, x, p, yb, w_out, gpost):
    tm = rows.tm
    row_tile = lambda width: pl.BlockSpec((tm, width), lambda i: (i, 0))
    return pl.pallas_call(
        functools.partial(_mixout_kernel, rows=rows),
        grid=(rows.ntiles,),
        in_specs=[row_tile(D_MODEL), row_tile(COL_TILE), row_tile(GDN_QK),
                  pl.BlockSpec((CONV_WIDTH + GDN_QK, D_MODEL), lambda i: (0, 0)),
                  pl.BlockSpec((1, D_MODEL), lambda i: (0, 0))],
        out_specs=row_tile(D_MODEL),
        out_shape=jax.ShapeDtypeStruct((rows.nrows, D_MODEL), F32),
        compiler_params=pltpu.CompilerParams(
            dimension_semantics=("arbitrary",), vmem_limit_bytes=VMEM_LIMIT_BYTES),
        name="mixout",
    )(x, p, yb, w_out, gpost)


def _ffn_kernel(x_ref, gpre_ref, wg_ref, wv_ref, wc_ref, wd_ref, gpost_ref, hist_ref,
                o_ref, nh_ref, h_sc, cbuf, uv_sc, act_sc, tail, *, rows):
    i = pl.program_id(0)
    c = pl.program_id(1)
    tm = rows.tm

    @pl.when(c == 0)
    def _():
        def norm(r0):
            blk = pl.ds(r0, BF16_ROWS)
            x = x_ref[blk, :]
            h_sc[blk, :] = (x * _rms_scale(x) * gpre_ref[...]).astype(BF16)
        _row_loop(tm, BF16_ROWS, norm)

    cbuf[pl.ds(SUBLANES, tm), :] = _dot(h_sc[...], wg_ref[...])
    uv_sc[...] = _dot(h_sc[...], wv_ref[...])
    _stage_history(cbuf, tail.at[c], hist_ref, rows, i, 2)

    def conv(r0):
        for t in range(FF_TILE // LANES):
            cols = slice(t * LANES, (t + 1) * LANES)
            gate = _silu(_conv_block(cbuf, r0, ROW_BLOCK, cols, wc_ref[:, cols]))
            act_sc[pl.ds(r0, ROW_BLOCK), cols] = (gate * uv_sc[pl.ds(r0, ROW_BLOCK), cols]).astype(BF16)
    _row_loop(tm, ROW_BLOCK, conv)
    _save_history(cbuf, tail.at[c], nh_ref, rows)

    down = _dot(act_sc[...], wd_ref[...])

    @pl.when(c == 0)
    def _():
        o_ref[...] = down

    @pl.when(c > 0)
    def _():
        o_ref[...] += down

    @pl.when(c == pl.num_programs(1) - 1)
    def _():
        def norm(r0):
            blk = pl.ds(r0, BF16_ROWS)
            f = o_ref[blk, :]
            o_ref[blk, :] = x_ref[blk, :] + f * _rms_scale(f) * gpost_ref[...]
        _row_loop(tm, BF16_ROWS, norm)


def _ffn(rows, x1, gpre, w_up, wcf, w_down, gpost, hist_f):
    tm, nseg = rows.tm, rows.nseg
    sb = functools.partial(_stream_block, rows)
    const = lambda i, c: (0, 0)
    hist = pl.BlockSpec((nseg, SUBLANES, FF_TILE), lambda i, c: (sb(i), 0, c))
    return pl.pallas_call(
        functools.partial(_ffn_kernel, rows=rows),
        grid=(rows.ntiles, N_FF_TILES),
        in_specs=[
            pl.BlockSpec((tm, D_MODEL), lambda i, c: (i, 0)),
            pl.BlockSpec((1, D_MODEL), const),
            pl.BlockSpec((D_MODEL, FF_TILE), lambda i, c: (0, c)),
            pl.BlockSpec((D_MODEL, FF_TILE), lambda i, c: (0, N_FF_TILES + c)),
            pl.BlockSpec((3, FF_TILE), lambda i, c: (0, c)),
            pl.BlockSpec((FF_TILE, D_MODEL), lambda i, c: (c, 0)),
            pl.BlockSpec((1, D_MODEL), const),
            hist,
        ],
        out_specs=[pl.BlockSpec((tm, D_MODEL), lambda i, c: (i, 0)),
                   pl.BlockSpec((nseg, SUBLANES, FF_TILE), lambda i, c: (i, 0, c))],
        out_shape=[jax.ShapeDtypeStruct((rows.nrows, D_MODEL), F32),
                   jax.ShapeDtypeStruct((rows.ntiles * nseg, SUBLANES, D_FF), F32)],
        scratch_shapes=[
            pltpu.VMEM((tm, D_MODEL), BF16),
            pltpu.VMEM((tm + SUBLANES, FF_TILE), F32),
            pltpu.VMEM((tm, FF_TILE), F32),
            pltpu.VMEM((tm, FF_TILE), BF16),
            pltpu.VMEM((N_FF_TILES, SUBLANES, FF_TILE), F32),
        ],
        compiler_params=pltpu.CompilerParams(
            dimension_semantics=("arbitrary", "arbitrary"), vmem_limit_bytes=VMEM_LIMIT_BYTES),
        name="ffn",
    )(x1, gpre, w_up, w_up, wcf, w_down, gpost, hist_f)


def _history_block(state, nhist):
    return jnp.pad(state.astype(F32), ((0, 0), (SUBLANES - nhist, 0), (0, 0)))


def _run_group(x, lead, hist_a, hist_q, s0, hist_f, wts):
    nbatch, t, _ = x.shape
    nlead = lead.shape[0]
    tp = FRONT + nlead + t
    assert tp % CHUNK == 0, "streams must end on a chunk boundary after FRONT no-op rows"
    rows = _pick_rows(nbatch, tp)
    front = jnp.zeros((nbatch, FRONT, D_MODEL), F32)
    xp = jnp.concatenate([front, jnp.broadcast_to(lead[None], (nbatch, nlead, D_MODEL)), x], axis=1)
    xp = xp.reshape(rows.nrows, D_MODEL)

    p, gates, nha, nhq = _inproj(rows, xp, wts["gpre_mix"], wts["w_main"], wts["w_ab"], wts["wca"],
                                 wts["gna"], wts["wcg"], wts["alog"], wts["dtb"],
                                 _history_block(hist_a, 2), _history_block(hist_q, 3))
    yb, s_new = _gdn(rows, p, gates, s0.astype(F32), wts["gn_gdn"])
    x1 = _mixout(rows, xp, p, yb, wts["w_out"], wts["gpost_mix"])
    y, nhf = _ffn(rows, x1, wts["gpre_ffn"], wts["w_up"], wts["wcf"], wts["w_down"], wts["gpost_ffn"],
                  _history_block(hist_f, 2))
    y = y.reshape(nbatch, tp, D_MODEL)[:, tp - t:]
    last = lambda nh, n: nh.reshape(nbatch, -1, SUBLANES, nh.shape[-1])[None, :, -1, SUBLANES - n:]
    return (y, last(nha, 2), last(nhq, 3), s_new[None], last(nhf, 2))


def kernel(x_prompt, x_sample, state_conv_a, state_gdn_conv, state_gdn, state_ffn_conv, meta_tokens, g_pre_mix, w_in, w_conv_a, g_norm_a, w_conv_gdn, a_log, dt_bias, g_norm_gdn, w_out, g_post_mix, g_pre_ffn, w_up, w_conv_ffn, w_down, g_post_ffn):
    assert g_pre_mix.shape[0] == 1, "one layer"
    w_in0 = w_in[0]
    n_main = N_COL_TILES * COL_TILE
    lane_pad = lambda v: jnp.pad(v.astype(F32)[None], ((0, 0), (N_HEADS, LANES - 2 * N_HEADS)))
    wts = {
        "gpre_mix": g_pre_mix.astype(F32),
        "w_main": w_in0[:, :n_main].astype(BF16),
        "w_ab": jnp.pad(w_in0[:, n_main:], ((0, 0), (0, LANES - 2 * N_HEADS))).astype(BF16),
        "wca": w_conv_a[0].astype(F32),
        "gna": g_norm_a.astype(F32),
        "wcg": w_conv_gdn[0].astype(F32),
        "alog": lane_pad(a_log[0]),
        "dtb": lane_pad(dt_bias[0]),
        "gn_gdn": g_norm_gdn.astype(F32),
        "w_out": w_out[0].astype(BF16),
        "gpost_mix": g_post_mix.astype(F32),
        "gpre_ffn": g_pre_ffn.astype(F32),
        "w_up": w_up[0].astype(BF16),
        "wcf": w_conv_ffn[0].astype(F32),
        "w_down": w_down[0].astype(BF16),
        "gpost_ffn": g_post_ffn.astype(F32),
    }
    nb = x_prompt.shape[0]
    zeros = lambda *shape: jnp.zeros(shape, F32)
    yp, nca_p, ngc_p, ngd_p, nfc_p = _run_group(
        x_prompt, meta_tokens.astype(F32),
        zeros(nb, 2, CONV_WIDTH), zeros(nb, 3, QKV_WIDTH),
        zeros(nb, N_HEADS, HEAD_DIM, HEAD_DIM), zeros(nb, 2, D_FF), wts)
    ys, nca_s, ngc_s, ngd_s, nfc_s = _run_group(
        x_sample, jnp.zeros((0, D_MODEL), F32),
        state_conv_a[0], state_gdn_conv[0], state_gdn[0], state_ffn_conv[0], wts)
    return (yp, ys, nca_p, ngc_p, ngd_p, nfc_p, nca_s, ngc_s, ngd_s, nfc_s)
```

```python
import functools
from typing import NamedTuple

import jax
import jax.numpy as jnp
from jax import lax
from jax.experimental import pallas as pl
from jax.experimental.pallas import tpu as pltpu

F32 = jnp.float32
BF16 = jnp.bfloat16

D_MODEL = 2048
CHUNK = 64
N_META = 16
FRONT = CHUNK - N_META
CONV_WIDTH = 1024
N_HEADS = 8
HEAD_DIM = 128
GDN_QK = N_HEADS * HEAD_DIM
QKV_WIDTH = 3 * GDN_QK
D_FF = 5632
EPS = 1e-6

LANES = 128
SUBLANES = 8
BF16_ROWS = 16
VMEM_LIMIT_BYTES = 56 * 1024 * 1024

COL_TILE = 1024
N_COL_TILES = 7
P_TILES = 5
FF_TILE = 512
N_FF_TILES = D_FF // FF_TILE
MAX_TILE_ROWS = 1024
ROW_BLOCK = 32


class Rows(NamedTuple):
    nbatch: int
    tp: int
    tm: int
    nseg: int

    @property
    def tiles_per_stream(self):
        return self.tp // self.tm if self.nseg == 1 else 1

    @property
    def ntiles(self):
        return self.nbatch * self.tp // self.tm

    @property
    def nrows(self):
        return self.nbatch * self.tp


def _pick_rows(nbatch, tp):
    if tp * nbatch <= MAX_TILE_ROWS:
        return Rows(nbatch, tp, nbatch * tp, nbatch)
    tm = max(t for t in range(ROW_BLOCK, min(tp, MAX_TILE_ROWS) + 1, ROW_BLOCK) if tp % t == 0)
    return Rows(nbatch, tp, tm, 1)


def _stream_block(rows, i):
    return i // rows.tiles_per_stream if rows.nseg == 1 else i


def _is_first_tile(rows, i):
    return (i % rows.tiles_per_stream) == 0 if rows.nseg == 1 else None


def _noop_rows(rows, i, shape, row0=0, enable=True):
    r = lax.broadcasted_iota(jnp.int32, shape, 0) + row0
    if rows.nseg == 1:
        enable = jnp.logical_and(enable, _is_first_tile(rows, i))
    else:
        r = r % rows.tp
    return r < jnp.where(enable, FRONT, 0)


def _rms_scale(x):
    return lax.rsqrt(jnp.mean(x * x, axis=-1, keepdims=True) + EPS)


def _silu(x):
    return x * jax.nn.sigmoid(x)


def _stage_history(cbuf, tail, hist_ref, rows, i, nhist):
    width = cbuf.shape[1]
    srow = lax.broadcasted_iota(jnp.int32, (SUBLANES, width), 0)
    if rows.nseg > 1:
        cbuf[0:SUBLANES, :] = jnp.zeros((SUBLANES, width), F32)
        hist_rows = srow >= SUBLANES - nhist
    else:
        first = _is_first_tile(rows, i)
        cbuf[0:SUBLANES, :] = jnp.where(first, 0.0, tail[...])
        hist_rows = srow >= jnp.where(first, SUBLANES - nhist, SUBLANES)
    for s in range(rows.nseg):
        blk = pl.ds(s * rows.tp + FRONT, SUBLANES)
        cbuf[blk, :] = jnp.where(hist_rows, hist_ref[s], cbuf[blk, :])


def _save_history(cbuf, tail, nh_ref, rows):
    if rows.nseg == 1:
        last = cbuf[pl.ds(rows.tm, SUBLANES), :]
        tail[...] = last
        nh_ref[0] = last
    else:
        for s in range(rows.nseg):
            nh_ref[s] = cbuf[pl.ds((s + 1) * rows.tp, SUBLANES), :]


def _conv_block(cbuf, r0, nrows, cols, w):
    k = w.shape[0]
    xb = cbuf[pl.ds(r0, nrows + SUBLANES), cols]
    y = xb[SUBLANES:] * w[k - 1:k]
    for s in range(1, k):
        y = y + pltpu.roll(xb, s, axis=0)[SUBLANES:] * w[k - 1 - s:k - s]
    return y


def _row_loop(nrows, step, body, unroll=1):
    def wrapped(b, carry):
        body(pl.multiple_of(b * step, step))
        return carry
    lax.fori_loop(0, nrows // step, wrapped, 0, unroll=unroll)


NORM_UNROLL = 4


def _residual_norm(o_ref, x_ref, g_ref, nrows):
    for r0 in range(0, nrows, SUBLANES):
        blk = slice(r0, r0 + SUBLANES)
        f = o_ref[blk, :]
        o_ref[blk, :] = x_ref[blk, :] + f * _rms_scale(f) * g_ref[...]


def _inproj_kernel(x_ref, gpre_ref, w_ref, wab_ref, wca_ref, gna_ref, wcg_ref, alog_ref, dtb_ref,
                   hista_ref, histq_ref,
                   p_ref, gates_ref, nha_ref, nhq_ref,
                   h_sc, ah_sc, cbuf, tail_a, tail_q, *, rows):
    i = pl.program_id(0)
    j = pl.program_id(1)
    tm = rows.tm
    tile = pl.ds(SUBLANES, tm)

    @pl.when(j == 0)
    def _():
        @pl.when(i == 0)
        def _():
            tail_a[...] = jnp.zeros(tail_a.shape, F32)
            tail_q[...] = jnp.zeros(tail_q.shape, F32)

        def norm(r0):
            x = x_ref[pl.ds(r0, BF16_ROWS), :]
            h_sc[pl.ds(r0, BF16_ROWS), :] = (x * _rms_scale(x) * gpre_ref[...]).astype(BF16)
        _row_loop(tm, BF16_ROWS, norm, NORM_UNROLL)

        logits = jnp.dot(h_sc[...], wab_ref[...], preferred_element_type=F32)
        lane = lax.broadcasted_iota(jnp.int32, (tm, LANES), 1)
        beta = jax.nn.sigmoid(logits)
        g = -jnp.exp(alog_ref[...]) * jax.nn.softplus(logits + dtb_ref[...])
        val = jnp.where(lane < N_HEADS, beta, g)
        val = jnp.where(lane < 2 * N_HEADS, val, 0.0)
        gates_ref[...] = jnp.where(_noop_rows(rows, i, (tm, LANES)), 0.0, val)

        ah_sc[...] = jnp.dot(h_sc[...], w_ref[...], preferred_element_type=F32)

    @pl.when(j > 0)
    def _():
        cbuf[tile, :] = jnp.dot(h_sc[...], w_ref[...], preferred_element_type=F32)

    @pl.when(j == 1)
    def _():
        def gate(r0):
            blk = pl.ds(r0 + SUBLANES, ROW_BLOCK)
            cbuf[blk, :] = cbuf[blk, :] * ah_sc[pl.ds(r0, ROW_BLOCK), :]
        _row_loop(tm, ROW_BLOCK, gate)
        _stage_history(cbuf, tail_a, hista_ref, rows, i, 2)

        def conv(r0):
            for c in range(COL_TILE // LANES):
                cols = slice(c * LANES, (c + 1) * LANES)
                ah_sc[pl.ds(r0, ROW_BLOCK), cols] = _conv_block(cbuf, r0, ROW_BLOCK, cols, wca_ref[:, cols])
        _row_loop(tm, ROW_BLOCK, conv)
        _save_history(cbuf, tail_a, nha_ref, rows)

    @pl.when(j == 2)
    def _():
        def norm(r0):
            y = cbuf[pl.ds(r0 + SUBLANES, BF16_ROWS), :] * ah_sc[pl.ds(r0, BF16_ROWS), :]
            p_ref[pl.ds(r0, BF16_ROWS), :] = (y * _rms_scale(y) * gna_ref[...]).astype(BF16)
        _row_loop(tm, BF16_ROWS, norm, NORM_UNROLL)

    @pl.when(jnp.logical_and(j >= 3, j <= 5))
    def _():
        part = j - 3
        _stage_history(cbuf, tail_q.at[part], histq_ref, rows, i, 3)
        qscale = jnp.where(part == 0, HEAD_DIM ** -0.5, 1.0)

        def conv(r0):
            zero_k = _noop_rows(rows, i, (ROW_BLOCK, LANES), r0, enable=part == 1)
            for c in range(COL_TILE // LANES):
                cols = slice(c * LANES, (c + 1) * LANES)
                s = _silu(_conv_block(cbuf, r0, ROW_BLOCK, cols, wcg_ref[:, cols]))
                l2 = lax.rsqrt(jnp.sum(s * s, axis=-1, keepdims=True) + EPS) * qscale
                s = s * jnp.where(part == 2, 1.0, l2)
                p_ref[pl.ds(r0, ROW_BLOCK), cols] = jnp.where(zero_k, 0.0, s).astype(BF16)
        _row_loop(tm, ROW_BLOCK, conv)
        _save_history(cbuf, tail_q.at[part], nhq_ref, rows)

    @pl.when(j == 6)
    def _():
        def gate(r0):
            p_ref[pl.ds(r0, ROW_BLOCK), :] = _silu(cbuf[pl.ds(r0 + SUBLANES, ROW_BLOCK), :]).astype(BF16)
        _row_loop(tm, ROW_BLOCK, gate)


def _inproj(rows, x, gpre, w_main, w_ab, wca, gna, wcg, alog, dtb, hist_a, hist_q):
    tm, nseg = rows.tm, rows.nseg
    sb = functools.partial(_stream_block, rows)
    qpart = lambda j: jnp.clip(j - 3, 0, 2)
    const = lambda i, j: (0, 0)
    return pl.pallas_call(
        functools.partial(_inproj_kernel, rows=rows),
        grid=(rows.ntiles, N_COL_TILES),
        in_specs=[
            pl.BlockSpec((tm, D_MODEL), lambda i, j: (i, 0)),
            pl.BlockSpec((1, D_MODEL), const),
            pl.BlockSpec((D_MODEL, COL_TILE), lambda i, j: (0, j)),
            pl.BlockSpec((D_MODEL, LANES), const),
            pl.BlockSpec((3, CONV_WIDTH), const),
            pl.BlockSpec((1, CONV_WIDTH), const),
            pl.BlockSpec((4, COL_TILE), lambda i, j: (0, qpart(j))),
            pl.BlockSpec((1, LANES), const),
            pl.BlockSpec((1, LANES), const),
            pl.BlockSpec((nseg, SUBLANES, CONV_WIDTH), lambda i, j: (sb(i), 0, 0)),
            pl.BlockSpec((nseg, SUBLANES, COL_TILE), lambda i, j: (sb(i), 0, qpart(j))),
        ],
        out_specs=[
            pl.BlockSpec((tm, COL_TILE), lambda i, j: (i, jnp.maximum(j - 2, 0))),
            pl.BlockSpec((tm, LANES), lambda i, j: (i, 0)),
            pl.BlockSpec((nseg, SUBLANES, CONV_WIDTH), lambda i, j: (i, 0, 0)),
            pl.BlockSpec((nseg, SUBLANES, COL_TILE), lambda i, j: (i, 0, qpart(j))),
        ],
        out_shape=[
            jax.ShapeDtypeStruct((rows.nrows, P_TILES * COL_TILE), BF16),
            jax.ShapeDtypeStruct((rows.nrows, LANES), F32),
            jax.ShapeDtypeStruct((rows.ntiles * nseg, SUBLANES, CONV_WIDTH), F32),
            jax.ShapeDtypeStruct((rows.ntiles * nseg, SUBLANES, QKV_WIDTH), F32),
        ],
        scratch_shapes=[
            pltpu.VMEM((tm, D_MODEL), BF16),
            pltpu.VMEM((tm, COL_TILE), F32),
            pltpu.VMEM((tm + SUBLANES, COL_TILE), F32),
            pltpu.VMEM((SUBLANES, CONV_WIDTH), F32),
            pltpu.VMEM((3, SUBLANES, COL_TILE), F32),
        ],
        compiler_params=pltpu.CompilerParams(
            dimension_semantics=("arbitrary", "arbitrary"), vmem_limit_bytes=VMEM_LIMIT_BYTES),
        name="inproj",
    )(x, gpre, w_main, w_ab, wca, gna, wcg, alog, dtb, hist_a, hist_q)


def _dot(a, b):
    return jnp.dot(a, b, preferred_element_type=F32)


def _dot_nt(a, b):
    return lax.dot_general(a, b, (((1,), (1,)), ((), ())), preferred_element_type=F32)


def _dot_tn(a, b):
    return lax.dot_general(a, b, (((0,), (0,)), ((), ())), preferred_element_type=F32)


def _gdn_kernel(q_ref, k_ref, v_ref, zs_ref, gates_ref, s0_ref, gn_ref, yb_ref, sout_ref, s_sc):
    n = pl.program_id(1)

    @pl.when(n == 0)
    def _():
        s_sc[...] = s0_ref[0]

    gates = gates_ref[...]
    ri = lax.broadcasted_iota(jnp.int32, (CHUNK, CHUNK), 0)
    ci = lax.broadcasted_iota(jnp.int32, (CHUNK, CHUNK), 1)
    lower = ri >= ci
    strict = ri > ci
    eye = (ri == ci).astype(F32)

    tri = lower.astype(BF16)
    g_hi = gates.astype(BF16)
    rem = gates - g_hi.astype(F32)
    g_mid = rem.astype(BF16)
    g_lo = (rem - g_mid.astype(F32)).astype(BF16)
    gcum = _dot(tri, g_hi) + _dot(tri, g_mid) + _dot(tri, g_lo)
    gcum_t = gcum.T
    gates_t = gates.T
    g_last = gcum[CHUNK - 1:CHUNK, :]
    e_cum = jnp.exp(gcum)
    e_cum_t = jnp.exp(gcum_t)
    e_rest = jnp.exp(g_last - gcum)
    e_last = jnp.exp(g_last)

    heads = range(N_HEADS)
    cols = [slice(h * HEAD_DIM, (h + 1) * HEAD_DIM) for h in heads]
    gl = [N_HEADS + h for h in heads]
    q = [q_ref[:, c] for c in cols]
    k = [k_ref[:, c] for c in cols]
    v = [v_ref[:, c] for c in cols]
    s = [s_sc[h] for h in heads]
    s_b = [x.astype(BF16) for x in s]

    kq = [_dot_nt(jnp.concatenate([k[h], q[h]], axis=0), k[h]) for h in heads]
    qs = [_dot(q[h], s_b[h]) for h in heads]
    decay = [jnp.exp(jnp.where(lower, gcum[:, g:g + 1] - gcum_t[g:g + 1, :], -jnp.inf)) for g in gl]
    m = [gates[:, h:h + 1] * kq[h][:CHUNK] * jnp.where(strict, decay[h], 0.0) for h in heads]
    attn = [kq[h][CHUNK:] * decay[h] for h in heads]

    inv = [eye - x for x in m]
    pw = [x.astype(BF16) for x in m]
    for _ in range(5):
        pw = [_dot(x, x).astype(BF16) for x in pw]
        inv = [inv[h] + _dot(inv[h].astype(BF16), pw[h]) for h in heads]

    inv_b = [inv[h] * gates_t[h:h + 1, :] for h in heads]
    u = [_dot(inv_b[h].astype(BF16), v[h]) for h in heads]
    w = [_dot((inv_b[h] * e_cum_t[g:g + 1, :]).astype(BF16), k[h]) for h, g in zip(heads, gl)]
    v_new = [u[h] - _dot(w[h].astype(BF16), s_b[h]) for h in heads]
    av = [_dot(attn[h].astype(BF16), v_new[h].astype(BF16)) for h in heads]
    kv = [_dot_tn(k[h], (v_new[h] * e_rest[:, g:g + 1]).astype(BF16)) for h, g in zip(heads, gl)]

    for h, g in zip(heads, gl):
        s_sc[h] = s[h] * e_last[:, g:g + 1] + kv[h]
        o = e_cum[:, g:g + 1] * qs[h] + av[h]
        yb = o * _rms_scale(o) * gn_ref[...] * zs_ref[:, cols[h]].astype(F32)
        yb_ref[:, cols[h]] = yb.astype(BF16)

    @pl.when(n == pl.num_programs(1) - 1)
    def _():
        sout_ref[0] = s_sc[...]


def _gdn(rows, p, gates, s0, gn):
    nc = rows.tp // CHUNK
    blk = lambda t: pl.BlockSpec((CHUNK, COL_TILE), lambda b, n: (b * nc + n, t))
    state = pl.BlockSpec((1, N_HEADS, HEAD_DIM, HEAD_DIM), lambda b, n: (b, 0, 0, 0))
    return pl.pallas_call(
        _gdn_kernel,
        grid=(rows.nbatch, nc),
        in_specs=[blk(1), blk(2), blk(3), blk(4),
                  pl.BlockSpec((CHUNK, LANES), lambda b, n: (b * nc + n, 0)),
                  state,
                  pl.BlockSpec((1, HEAD_DIM), lambda b, n: (0, 0))],
        out_specs=[pl.BlockSpec((CHUNK, GDN_QK), lambda b, n: (b * nc + n, 0)), state],
        out_shape=[jax.ShapeDtypeStruct((rows.nrows, GDN_QK), BF16),
                   jax.ShapeDtypeStruct((rows.nbatch, N_HEADS, HEAD_DIM, HEAD_DIM), F32)],
        scratch_shapes=[pltpu.VMEM((N_HEADS, HEAD_DIM, HEAD_DIM), F32)],
        compiler_params=pltpu.CompilerParams(
            dimension_semantics=("arbitrary", "arbitrary"), vmem_limit_bytes=VMEM_LIMIT_BYTES),
        name="gdn",
    )(p, p, p, p, gates, s0, gn)


def _mixout_kernel(x_ref, ya_ref, yb_ref, w_ref, g_ref, o_ref, *, rows):
    o_ref[...] = (_dot(ya_ref[...], w_ref[0:CONV_WIDTH, :])
                  + _dot(yb_ref[...], w_ref[CONV_WIDTH:CONV_WIDTH + GDN_QK, :]))

    _residual_norm(o_ref, x_ref, g_ref, rows.tm)


def _mixout(rows, x, p, yb, w_out, gpost):
    tm = rows.tm
    row_tile = lambda width: pl.BlockSpec((tm, width), lambda i: (i, 0))
    return pl.pallas_call(
        functools.partial(_mixout_kernel, rows=rows),
        grid=(rows.ntiles,),
        in_specs=[row_tile(D_MODEL), row_tile(COL_TILE), row_tile(GDN_QK),
                  pl.BlockSpec((CONV_WIDTH + GDN_QK, D_MODEL), lambda i: (0, 0)),
                  pl.BlockSpec((1, D_MODEL), lambda i: (0, 0))],
        out_specs=row_tile(D_MODEL),
        out_shape=jax.ShapeDtypeStruct((rows.nrows, D_MODEL), F32),
        compiler_params=pltpu.CompilerParams(
            dimension_semantics=("arbitrary",), vmem_limit_bytes=VMEM_LIMIT_BYTES),
        name="mixout",
    )(x, p, yb, w_out, gpost)


def _ffn_kernel(x_ref, gpre_ref, wg_ref, wv_ref, wc_ref, wd_ref, gpost_ref, hist_ref,
                o_ref, nh_ref, h_sc, cbuf, uv_sc, act_sc, tail, *, rows):
    i = pl.program_id(0)
    c = pl.program_id(1)
    tm = rows.tm

    half = tm // 2

    @pl.when(c == 0)
    def _():
        @pl.when(i == 0)
        def _():
            tail[...] = jnp.zeros(tail.shape, F32)

        def norm(r0):
            blk = pl.ds(r0, BF16_ROWS)
            x = x_ref[blk, :]
            h_sc[blk, :] = (x * _rms_scale(x) * gpre_ref[...]).astype(BF16)
            o_ref[blk, :] = jnp.zeros((BF16_ROWS, D_MODEL), F32)
        _row_loop(tm, BF16_ROWS, norm, NORM_UNROLL)

    for r0 in (0, half):
        h = h_sc[r0:r0 + half, :]
        cbuf[SUBLANES + r0:SUBLANES + r0 + half, :] = _dot(h, wg_ref[...])
        uv_sc[r0:r0 + half, :] = _dot(h, wv_ref[...])
    _stage_history(cbuf, tail.at[c], hist_ref, rows, i, 2)
    for r0 in (0, half):
        for rb in range(r0, r0 + half, ROW_BLOCK):
            blk = slice(rb, rb + ROW_BLOCK)
            for t in range(FF_TILE // LANES):
                cols = slice(t * LANES, (t + 1) * LANES)
                gate = _silu(_conv_block(cbuf, rb, ROW_BLOCK, cols, wc_ref[:, cols]))
                act_sc[blk, cols] = (gate * uv_sc[blk, cols]).astype(BF16)
        o_ref[r0:r0 + half, :] += _dot(act_sc[r0:r0 + half, :], wd_ref[...])
    _save_history(cbuf, tail.at[c], nh_ref, rows)

    @pl.when(c == pl.num_programs(1) - 1)
    def _():
        _residual_norm(o_ref, x_ref, gpost_ref, tm)


def _ffn(rows, x1, gpre, w_up, wcf, w_down, gpost, hist_f):
    tm, nseg = rows.tm, rows.nseg
    sb = functools.partial(_stream_block, rows)
    const = lambda i, c: (0, 0)
    hist = pl.BlockSpec((nseg, SUBLANES, FF_TILE), lambda i, c: (sb(i), 0, c))
    return pl.pallas_call(
        functools.partial(_ffn_kernel, rows=rows),
        grid=(rows.ntiles, N_FF_TILES),
        in_specs=[
            pl.BlockSpec((tm, D_MODEL), lambda i, c: (i, 0)),
            pl.BlockSpec((1, D_MODEL), const),
            pl.BlockSpec((D_MODEL, FF_TILE), lambda i, c: (0, c)),
            pl.BlockSpec((D_MODEL, FF_TILE), lambda i, c: (0, N_FF_TILES + c)),
            pl.BlockSpec((3, FF_TILE), lambda i, c: (0, c)),
            pl.BlockSpec((FF_TILE, D_MODEL), lambda i, c: (c, 0)),
            pl.BlockSpec((1, D_MODEL), const),
            hist,
        ],
        out_specs=[pl.BlockSpec((tm, D_MODEL), lambda i, c: (i, 0)),
                   pl.BlockSpec((nseg, SUBLANES, FF_TILE), lambda i, c: (i, 0, c))],
        out_shape=[jax.ShapeDtypeStruct((rows.nrows, D_MODEL), F32),
                   jax.ShapeDtypeStruct((rows.ntiles * nseg, SUBLANES, D_FF), F32)],
        scratch_shapes=[
            pltpu.VMEM((tm, D_MODEL), BF16),
            pltpu.VMEM((tm + SUBLANES, FF_TILE), F32),
            pltpu.VMEM((tm, FF_TILE), F32),
            pltpu.VMEM((tm, FF_TILE), BF16),
            pltpu.VMEM((N_FF_TILES, SUBLANES, FF_TILE), F32),
        ],
        compiler_params=pltpu.CompilerParams(
            dimension_semantics=("arbitrary", "arbitrary"), vmem_limit_bytes=VMEM_LIMIT_BYTES),
        name="ffn",
    )(x1, gpre, w_up, w_up, wcf, w_down, gpost, hist_f)


def _history_block(state, nhist):
    return jnp.pad(state.astype(F32), ((0, 0), (SUBLANES - nhist, 0), (0, 0)))


def _run_group(x, lead, hist_a, hist_q, s0, hist_f, wts):
    nbatch, t, _ = x.shape
    nlead = lead.shape[0]
    tp = FRONT + nlead + t
    assert tp % CHUNK == 0, "streams must end on a chunk boundary after FRONT no-op rows"
    rows = _pick_rows(nbatch, tp)
    front = jnp.zeros((nbatch, FRONT, D_MODEL), F32)
    xp = jnp.concatenate([front, jnp.broadcast_to(lead[None], (nbatch, nlead, D_MODEL)), x], axis=1)
    xp = xp.reshape(rows.nrows, D_MODEL)

    p, gates, nha, nhq = _inproj(rows, xp, wts["gpre_mix"], wts["w_main"], wts["w_ab"], wts["wca"],
                                 wts["gna"], wts["wcg"], wts["alog"], wts["dtb"],
                                 _history_block(hist_a, 2), _history_block(hist_q, 3))
    yb, s_new = _gdn(rows, p, gates, s0.astype(F32), wts["gn_gdn"])
    x1 = _mixout(rows, xp, p, yb, wts["w_out"], wts["gpost_mix"])
    y, nhf = _ffn(rows, x1, wts["gpre_ffn"], wts["w_up"], wts["wcf"], wts["w_down"], wts["gpost_ffn"],
                  _history_block(hist_f, 2))
    y = y.reshape(nbatch, tp, D_MODEL)[:, tp - t:]
    last = lambda nh, n: nh.reshape(nbatch, -1, SUBLANES, nh.shape[-1])[None, :, -1, SUBLANES - n:]
    return (y, last(nha, 2), last(nhq, 3), s_new[None], last(nhf, 2))


def kernel(x_prompt, x_sample, state_conv_a, state_gdn_conv, state_gdn, state_ffn_conv, meta_tokens, g_pre_mix, w_in, w_conv_a, g_norm_a, w_conv_gdn, a_log, dt_bias, g_norm_gdn, w_out, g_post_mix, g_pre_ffn, w_up, w_conv_ffn, w_down, g_post_ffn):
    assert g_pre_mix.shape[0] == 1, "one layer"
    w_in0 = w_in[0]
    n_main = N_COL_TILES * COL_TILE
    lane_pad = lambda v: jnp.pad(v.astype(F32)[None], ((0, 0), (N_HEADS, LANES - 2 * N_HEADS)))
    wts = {
        "gpre_mix": g_pre_mix.astype(F32),
        "w_main": w_in0[:, :n_main].astype(BF16),
        "w_ab": jnp.pad(w_in0[:, n_main:], ((0, 0), (0, LANES - 2 * N_HEADS))).astype(BF16),
        "wca": w_conv_a[0].astype(F32),
        "gna": g_norm_a.astype(F32),
        "wcg": w_conv_gdn[0].astype(F32),
        "alog": lane_pad(a_log[0]),
        "dtb": lane_pad(dt_bias[0]),
        "gn_gdn": g_norm_gdn.astype(F32),
        "w_out": w_out[0].astype(BF16),
        "gpost_mix": g_post_mix.astype(F32),
        "gpre_ffn": g_pre_ffn.astype(F32),
        "w_up": w_up[0].astype(BF16),
        "wcf": w_conv_ffn[0].astype(F32),
        "w_down": w_down[0].astype(BF16),
        "gpost_ffn": g_post_ffn.astype(F32),
    }
    nb = x_prompt.shape[0]
    zeros = lambda *shape: jnp.zeros(shape, F32)
    yp, nca_p, ngc_p, ngd_p, nfc_p = _run_group(
        x_prompt, meta_tokens.astype(F32),
        zeros(nb, 2, CONV_WIDTH), zeros(nb, 3, QKV_WIDTH),
        zeros(nb, N_HEADS, HEAD_DIM, HEAD_DIM), zeros(nb, 2, D_FF), wts)
    ys, nca_s, ngc_s, ngd_s, nfc_s = _run_group(
        x_sample, jnp.zeros((0, D_MODEL), F32),
        state_conv_a[0], state_gdn_conv[0], state_gdn[0], state_ffn_conv[0], wts)
    return (yp, ys, nca_p, ngc_p, ngd_p, nfc_p, nca_s, ngc_s, ngd_s, nfc_s)
```

```python
import functools
from typing import NamedTuple

import jax
import jax.numpy as jnp
from jax import lax
from jax.experimental import pallas as pl
from jax.experimental.pallas import tpu as pltpu

F32 = jnp.float32
BF16 = jnp.bfloat16

D_MODEL = 2048
CHUNK = 64
N_META = 16
FRONT = CHUNK - N_META
CONV_WIDTH = 1024
N_HEADS = 8
HEAD_DIM = 128
GDN_QK = N_HEADS * HEAD_DIM
QKV_WIDTH = 3 * GDN_QK
D_FF = 5632
EPS = 1e-6

LANES = 128
SUBLANES = 8
BF16_ROWS = 16
VMEM_LIMIT_BYTES = 56 * 1024 * 1024

COL_TILE = 1024
N_COL_TILES = 7
P_TILES = 5
FF_TILE = 512
N_FF_TILES = D_FF // FF_TILE
MAX_TILE_ROWS = 1024
INPROJ_TILE_ROWS = 448
ROW_BLOCK = 32


class Rows(NamedTuple):
    nbatch: int
    tp: int
    tm: int
    nseg: int

    @property
    def tiles_per_stream(self):
        return self.tp // self.tm if self.nseg == 1 else 1

    @property
    def ntiles(self):
        return self.nbatch * self.tp // self.tm

    @property
    def nrows(self):
        return self.nbatch * self.tp


def _pick_rows(nbatch, tp, max_rows, multiple):
    if tp <= max_rows // 2:
        nseg = max(n for n in range(1, nbatch + 1) if nbatch % n == 0 and n * tp <= max_rows)
        return Rows(nbatch, tp, nseg * tp, nseg)
    tm = max(t for t in range(multiple, min(tp, max_rows) + 1, multiple) if tp % t == 0)
    return Rows(nbatch, tp, tm, 1)


def _stream_block(rows, i):
    return i // rows.tiles_per_stream if rows.nseg == 1 else i


def _is_first_tile(rows, i):
    return (i % rows.tiles_per_stream) == 0 if rows.nseg == 1 else None


def _noop_rows(rows, i, shape, row0=0, enable=True):
    r = lax.broadcasted_iota(jnp.int32, shape, 0) + row0
    if rows.nseg == 1:
        enable = jnp.logical_and(enable, _is_first_tile(rows, i))
    else:
        r = r % rows.tp
    return r < jnp.where(enable, FRONT, 0)


def _rms_scale(x):
    return lax.rsqrt(jnp.mean(x * x, axis=-1, keepdims=True) + EPS)


def _silu(x):
    return x * jax.nn.sigmoid(x)


def _stage_history(cbuf, tail, hist, rows, i, nhist):
    width = cbuf.shape[1]
    srow = lax.broadcasted_iota(jnp.int32, (SUBLANES, width), 0)
    if rows.nseg > 1:
        cbuf[0:SUBLANES, :] = jnp.zeros((SUBLANES, width), F32)
        hist_rows = srow >= SUBLANES - nhist
    else:
        first = _is_first_tile(rows, i)
        cbuf[0:SUBLANES, :] = jnp.where(first, 0.0, tail[...])
        hist_rows = srow >= jnp.where(first, SUBLANES - nhist, SUBLANES)
    for s in range(rows.nseg):
        blk = pl.ds(s * rows.tp + FRONT, SUBLANES)
        cbuf[blk, :] = jnp.where(hist_rows, hist(s), cbuf[blk, :])


def _save_history(cbuf, tail, put, rows):
    if rows.nseg == 1:
        last = cbuf[pl.ds(rows.tm, SUBLANES), :]
        tail[...] = last
        put(0, last)
    else:
        for s in range(rows.nseg):
            put(s, cbuf[pl.ds((s + 1) * rows.tp, SUBLANES), :])


def _conv_block(cbuf, r0, nrows, cols, w):
    k = w.shape[0]
    xb = cbuf[pl.ds(r0, nrows + SUBLANES), cols]
    y = xb[SUBLANES:] * w[k - 1:k]
    for s in range(1, k):
        y = y + pltpu.roll(xb, s, axis=0)[SUBLANES:] * w[k - 1 - s:k - s]
    return y


def _row_loop(nrows, step, body, unroll=1):
    def wrapped(b, carry):
        body(pl.multiple_of(b * step, step))
        return carry
    lax.fori_loop(0, nrows // step, wrapped, 0, unroll=unroll)


NORM_UNROLL = 4


def _residual_norm(o_ref, x_ref, g_ref, nrows):
    for r0 in range(0, nrows, SUBLANES):
        blk = slice(r0, r0 + SUBLANES)
        f = o_ref[blk, :]
        o_ref[blk, :] = x_ref[blk, :] + f * _rms_scale(f) * g_ref[...]


def _inproj_kernel(x_ref, gpre_ref, w_ref, wab_ref, wca_ref, gna_ref, wcg_ref, alog_ref, dtb_ref,
                   hista_ref, histq_ref,
                   p_ref, gates_ref, nha_ref, nhq_ref,
                   h_sc, ah_sc, cb0, cb1, tail_a, tail_q, *, rows):
    i = pl.program_id(0)
    tm = rows.tm
    tile = slice(SUBLANES, SUBLANES + tm)
    row_blocks = range(0, tm, ROW_BLOCK)
    lane_blocks = [slice(c * LANES, (c + 1) * LANES) for c in range(COL_TILE // LANES)]

    @pl.when(i == 0)
    def _():
        tail_a[...] = jnp.zeros(tail_a.shape, F32)
        tail_q[...] = jnp.zeros(tail_q.shape, F32)

    for r0 in range(0, tm, BF16_ROWS):
        x = x_ref[r0:r0 + BF16_ROWS, :]
        h_sc[r0:r0 + BF16_ROWS, :] = (x * _rms_scale(x) * gpre_ref[...]).astype(BF16)

    def project(t):
        return _dot(h_sc[...], w_ref[:, t * COL_TILE:(t + 1) * COL_TILE])

    def p_cols(t, cols):
        return slice(t * COL_TILE + cols.start, t * COL_TILE + cols.stop)

    logits = _dot(h_sc[...], wab_ref[...])
    lane = lax.broadcasted_iota(jnp.int32, (tm, LANES), 1)
    g = -jnp.exp(alog_ref[...]) * jax.nn.softplus(logits + dtb_ref[...])
    val = jnp.where(lane < N_HEADS, jax.nn.sigmoid(logits), g)
    val = jnp.where(lane < 2 * N_HEADS, val, 0.0)
    gates_ref[...] = jnp.where(_noop_rows(rows, i, (tm, LANES)), 0.0, val)

    ah_sc[...] = project(0)
    cb0[tile, :] = project(1)
    for rb in row_blocks:
        blk = slice(rb + SUBLANES, rb + SUBLANES + ROW_BLOCK)
        cb0[blk, :] = cb0[blk, :] * ah_sc[rb:rb + ROW_BLOCK, :]
    _stage_history(cb0, tail_a, lambda s: hista_ref[s], rows, i, 2)
    for rb in row_blocks:
        for cols in lane_blocks:
            ah_sc[rb:rb + ROW_BLOCK, cols] = _conv_block(cb0, rb, ROW_BLOCK, cols, wca_ref[:, cols])
    _save_history(cb0, tail_a, nha_ref.__setitem__, rows)

    cb1[tile, :] = project(2)
    for r0 in range(0, tm, BF16_ROWS):
        y = cb1[r0 + SUBLANES:r0 + SUBLANES + BF16_ROWS, :] * ah_sc[r0:r0 + BF16_ROWS, :]
        p_ref[r0:r0 + BF16_ROWS, 0:COL_TILE] = (y * _rms_scale(y) * gna_ref[...]).astype(BF16)

    for part, cb in ((0, cb0), (1, cb1), (2, cb0)):
        wcols = lambda cols: p_cols(part, cols)
        cb[tile, :] = project(3 + part)
        _stage_history(cb, tail_q.at[part], lambda s: histq_ref[s, :, part * COL_TILE:(part + 1) * COL_TILE],
                       rows, i, 3)
        for rb in row_blocks:
            has_noop = part == 1 and (rows.nseg > 1 or rb < FRONT)
            for cols in lane_blocks:
                s = _silu(_conv_block(cb, rb, ROW_BLOCK, cols, wcg_ref[:, wcols(cols)]))
                if part < 2:
                    s = s * lax.rsqrt(jnp.sum(s * s, axis=-1, keepdims=True) + EPS)
                if part == 0:
                    s = s * HEAD_DIM ** -0.5
                if has_noop:
                    s = jnp.where(_noop_rows(rows, i, (ROW_BLOCK, LANES), rb), 0.0, s)
                p_ref[rb:rb + ROW_BLOCK, p_cols(1 + part, cols)] = s.astype(BF16)

        def put(s, last, part=part):
            nhq_ref[s, :, part * COL_TILE:(part + 1) * COL_TILE] = last
        _save_history(cb, tail_q.at[part], put, rows)

    cb1[tile, :] = project(6)
    for rb in row_blocks:
        z = cb1[rb + SUBLANES:rb + SUBLANES + ROW_BLOCK, :]
        p_ref[rb:rb + ROW_BLOCK, 4 * COL_TILE:5 * COL_TILE] = _silu(z).astype(BF16)


def _inproj(rows, x, gpre, w_main, w_ab, wca, gna, wcg, alog, dtb, hist_a, hist_q):
    tm, nseg = rows.tm, rows.nseg
    sb = functools.partial(_stream_block, rows)
    const = lambda i: (0, 0)
    resident = dict(pipeline_mode=pl.Buffered(1))
    return pl.pallas_call(
        functools.partial(_inproj_kernel, rows=rows),
        grid=(rows.ntiles,),
        in_specs=[
            pl.BlockSpec((tm, D_MODEL), lambda i: (i, 0)),
            pl.BlockSpec((1, D_MODEL), const),
            pl.BlockSpec((D_MODEL, N_COL_TILES * COL_TILE), const, **resident),
            pl.BlockSpec((D_MODEL, LANES), const, **resident),
            pl.BlockSpec((3, CONV_WIDTH), const),
            pl.BlockSpec((1, CONV_WIDTH), const),
            pl.BlockSpec((4, QKV_WIDTH), const),
            pl.BlockSpec((1, LANES), const),
            pl.BlockSpec((1, LANES), const),
            pl.BlockSpec((nseg, SUBLANES, CONV_WIDTH), lambda i: (sb(i), 0, 0)),
            pl.BlockSpec((nseg, SUBLANES, QKV_WIDTH), lambda i: (sb(i), 0, 0)),
        ],
        out_specs=[
            pl.BlockSpec((tm, P_TILES * COL_TILE), lambda i: (i, 0)),
            pl.BlockSpec((tm, LANES), lambda i: (i, 0)),
            pl.BlockSpec((nseg, SUBLANES, CONV_WIDTH), lambda i: (i, 0, 0)),
            pl.BlockSpec((nseg, SUBLANES, QKV_WIDTH), lambda i: (i, 0, 0)),
        ],
        out_shape=[
            jax.ShapeDtypeStruct((rows.nrows, P_TILES * COL_TILE), BF16),
            jax.ShapeDtypeStruct((rows.nrows, LANES), F32),
            jax.ShapeDtypeStruct((rows.ntiles * nseg, SUBLANES, CONV_WIDTH), F32),
            jax.ShapeDtypeStruct((rows.ntiles * nseg, SUBLANES, QKV_WIDTH), F32),
        ],
        scratch_shapes=[
            pltpu.VMEM((tm, D_MODEL), BF16),
            pltpu.VMEM((tm, COL_TILE), F32),
            pltpu.VMEM((tm + SUBLANES, COL_TILE), F32),
            pltpu.VMEM((tm + SUBLANES, COL_TILE), F32),
            pltpu.VMEM((SUBLANES, CONV_WIDTH), F32),
            pltpu.VMEM((3, SUBLANES, COL_TILE), F32),
        ],
        compiler_params=pltpu.CompilerParams(
            dimension_semantics=("arbitrary",), vmem_limit_bytes=VMEM_LIMIT_BYTES),
        name="inproj",
    )(x, gpre, w_main, w_ab, wca, gna, wcg, alog, dtb, hist_a, hist_q)


def _dot(a, b):
    return jnp.dot(a, b, preferred_element_type=F32)


def _dot_nt(a, b):
    return lax.dot_general(a, b, (((1,), (1,)), ((), ())), preferred_element_type=F32)


def _dot_tn(a, b):
    return lax.dot_general(a, b, (((0,), (0,)), ((), ())), preferred_element_type=F32)


def _gdn_kernel(q_ref, k_ref, v_ref, zs_ref, gates_ref, s0_ref, gn_ref, yb_ref, sout_ref, s_sc, *, nchunks):
    n = pl.program_id(1)

    @pl.when(n == 0)
    def _():
        s_sc[...] = s0_ref[0]

    gates = gates_ref[...]
    ri = lax.broadcasted_iota(jnp.int32, (CHUNK, CHUNK), 0)
    ci = lax.broadcasted_iota(jnp.int32, (CHUNK, CHUNK), 1)
    lower = ri >= ci
    strict = ri > ci
    eye = (ri == ci).astype(F32)

    tri = lower.astype(BF16)
    g_hi = gates.astype(BF16)
    rem = gates - g_hi.astype(F32)
    g_mid = rem.astype(BF16)
    g_lo = (rem - g_mid.astype(F32)).astype(BF16)
    gcum = _dot(tri, g_hi) + _dot(tri, g_mid) + _dot(tri, g_lo)
    gcum_t = gcum.T
    gates_t = gates.T
    g_last = gcum[CHUNK - 1:CHUNK, :]
    e_cum = jnp.exp(gcum)
    e_cum_t = jnp.exp(gcum_t)
    e_rest = jnp.exp(g_last - gcum)
    e_last = jnp.exp(g_last)

    heads = range(N_HEADS)
    cols = [slice(h * HEAD_DIM, (h + 1) * HEAD_DIM) for h in heads]
    gl = [N_HEADS + h for h in heads]
    q = [q_ref[:, c] for c in cols]
    k = [k_ref[:, c] for c in cols]
    v = [v_ref[:, c] for c in cols]
    s = [s_sc[h] for h in heads]
    s_b = [x.astype(BF16) for x in s]

    kq = [_dot_nt(jnp.concatenate([k[h], q[h]], axis=0), k[h]) for h in heads]
    qs = [_dot(q[h], s_b[h]) for h in heads]
    decay = [jnp.exp(jnp.where(lower, gcum[:, g:g + 1] - gcum_t[g:g + 1, :], -jnp.inf)) for g in gl]
    m = [gates[:, h:h + 1] * kq[h][:CHUNK] * jnp.where(strict, decay[h], 0.0) for h in heads]
    attn = [kq[h][CHUNK:] * decay[h] for h in heads]

    inv = [eye - x for x in m]
    pw = [x.astype(BF16) for x in m]
    for _ in range(5):
        pw = [_dot(x, x).astype(BF16) for x in pw]
        inv = [inv[h] + _dot(inv[h].astype(BF16), pw[h]) for h in heads]

    inv_b = [inv[h] * gates_t[h:h + 1, :] for h in heads]
    u = [_dot(inv_b[h].astype(BF16), v[h]) for h in heads]
    w = [_dot((inv_b[h] * e_cum_t[g:g + 1, :]).astype(BF16), k[h]) for h, g in zip(heads, gl)]
    v_new = [u[h] - _dot(w[h].astype(BF16), s_b[h]) for h in heads]
    av = [_dot(attn[h].astype(BF16), v_new[h].astype(BF16)) for h in heads]
    kv = [_dot_tn(k[h], (v_new[h] * e_rest[:, g:g + 1]).astype(BF16)) for h, g in zip(heads, gl)]

    for h, g in zip(heads, gl):
        s_sc[h] = s[h] * e_last[:, g:g + 1] + kv[h]
        o = e_cum[:, g:g + 1] * qs[h] + av[h]
        yb = o * _rms_scale(o) * gn_ref[...] * zs_ref[:, cols[h]].astype(F32)
        yb_ref[:, cols[h]] = yb.astype(BF16)

    @pl.when(n == nchunks - 1)
    def _():
        sout_ref[0] = s_sc[...]


def _gdn(rows, p, gates, s0, gn):
    nc = rows.tp // CHUNK
    blk = lambda t: pl.BlockSpec((CHUNK, COL_TILE), lambda b, n: (b * nc + n, t))
    state = pl.BlockSpec((1, N_HEADS, HEAD_DIM, HEAD_DIM), lambda b, n: (b, 0, 0, 0))
    return pl.pallas_call(
        functools.partial(_gdn_kernel, nchunks=nc),
        grid=(rows.nbatch, nc),
        in_specs=[blk(1), blk(2), blk(3), blk(4),
                  pl.BlockSpec((CHUNK, LANES), lambda b, n: (b * nc + n, 0)),
                  state,
                  pl.BlockSpec((1, HEAD_DIM), lambda b, n: (0, 0))],
        out_specs=[pl.BlockSpec((CHUNK, GDN_QK), lambda b, n: (b * nc + n, 0)), state],
        out_shape=[jax.ShapeDtypeStruct((rows.nrows, GDN_QK), BF16),
                   jax.ShapeDtypeStruct((rows.nbatch, N_HEADS, HEAD_DIM, HEAD_DIM), F32)],
        scratch_shapes=[pltpu.VMEM((N_HEADS, HEAD_DIM, HEAD_DIM), F32)],
        compiler_params=pltpu.CompilerParams(
            dimension_semantics=("arbitrary", "arbitrary"), vmem_limit_bytes=VMEM_LIMIT_BYTES),
        name="gdn",
    )(p, p, p, p, gates, s0, gn)


def _mixout_kernel(x_ref, ya_ref, yb_ref, w_ref, g_ref, o_ref, *, rows):
    o_ref[...] = (_dot(ya_ref[...], w_ref[0:CONV_WIDTH, :])
                  + _dot(yb_ref[...], w_ref[CONV_WIDTH:CONV_WIDTH + GDN_QK, :]))

    _residual_norm(o_ref, x_ref, g_ref, rows.tm)


def _mixout(rows, x, p, yb, w_out, gpost):
    tm = rows.tm
    row_tile = lambda width: pl.BlockSpec((tm, width), lambda i: (i, 0))
    return pl.pallas_call(
        functools.partial(_mixout_kernel, rows=rows),
        grid=(rows.ntiles,),
        in_specs=[row_tile(D_MODEL), row_tile(COL_TILE), row_tile(GDN_QK),
                  pl.BlockSpec((CONV_WIDTH + GDN_QK, D_MODEL), lambda i: (0, 0)),
                  pl.BlockSpec((1, D_MODEL), lambda i: (0, 0))],
        out_specs=row_tile(D_MODEL),
        out_shape=jax.ShapeDtypeStruct((rows.nrows, D_MODEL), F32),
        compiler_params=pltpu.CompilerParams(
            dimension_semantics=("arbitrary",), vmem_limit_bytes=VMEM_LIMIT_BYTES),
        name="mixout",
    )(x, p, yb, w_out, gpost)


def _ffn_kernel(x_ref, gpre_ref, wg_ref, wv_ref, wc_ref, wd_ref, gpost_ref, hist_ref,
                o_ref, nh_ref, h_sc, cbuf, uv_sc, act_sc, tail, *, rows):
    i = pl.program_id(0)
    c = pl.program_id(1)
    tm = rows.tm

    half = tm // 2

    @pl.when(c == 0)
    def _():
        @pl.when(i == 0)
        def _():
            tail[...] = jnp.zeros(tail.shape, F32)

        def norm(r0):
            blk = pl.ds(r0, BF16_ROWS)
            x = x_ref[blk, :]
            h_sc[blk, :] = (x * _rms_scale(x) * gpre_ref[...]).astype(BF16)
            o_ref[blk, :] = jnp.zeros((BF16_ROWS, D_MODEL), F32)
        _row_loop(tm, BF16_ROWS, norm, NORM_UNROLL)

    for r0 in (0, half):
        h = h_sc[r0:r0 + half, :]
        cbuf[SUBLANES + r0:SUBLANES + r0 + half, :] = _dot(h, wg_ref[...])
        uv_sc[r0:r0 + half, :] = _dot(h, wv_ref[...])
    _stage_history(cbuf, tail.at[c], lambda s: hist_ref[s], rows, i, 2)
    for r0 in (0, half):
        for rb in range(r0, r0 + half, ROW_BLOCK):
            blk = slice(rb, rb + ROW_BLOCK)
            for t in range(FF_TILE // LANES):
                cols = slice(t * LANES, (t + 1) * LANES)
                gate = _silu(_conv_block(cbuf, rb, ROW_BLOCK, cols, wc_ref[:, cols]))
                act_sc[blk, cols] = (gate * uv_sc[blk, cols]).astype(BF16)
        o_ref[r0:r0 + half, :] += _dot(act_sc[r0:r0 + half, :], wd_ref[...])
    _save_history(cbuf, tail.at[c], nh_ref.__setitem__, rows)

    @pl.when(c == pl.num_programs(1) - 1)
    def _():
        _residual_norm(o_ref, x_ref, gpost_ref, tm)


def _ffn(rows, x1, gpre, w_up, wcf, w_down, gpost, hist_f):
    tm, nseg = rows.tm, rows.nseg
    sb = functools.partial(_stream_block, rows)
    const = lambda i, c: (0, 0)
    hist = pl.BlockSpec((nseg, SUBLANES, FF_TILE), lambda i, c: (sb(i), 0, c))
    return pl.pallas_call(
        functools.partial(_ffn_kernel, rows=rows),
        grid=(rows.ntiles, N_FF_TILES),
        in_specs=[
            pl.BlockSpec((tm, D_MODEL), lambda i, c: (i, 0)),
            pl.BlockSpec((1, D_MODEL), const),
            pl.BlockSpec((D_MODEL, FF_TILE), lambda i, c: (0, c)),
            pl.BlockSpec((D_MODEL, FF_TILE), lambda i, c: (0, N_FF_TILES + c)),
            pl.BlockSpec((3, FF_TILE), lambda i, c: (0, c)),
            pl.BlockSpec((FF_TILE, D_MODEL), lambda i, c: (c, 0)),
            pl.BlockSpec((1, D_MODEL), const),
            hist,
        ],
        out_specs=[pl.BlockSpec((tm, D_MODEL), lambda i, c: (i, 0)),
                   pl.BlockSpec((nseg, SUBLANES, FF_TILE), lambda i, c: (i, 0, c))],
        out_shape=[jax.ShapeDtypeStruct((rows.nrows, D_MODEL), F32),
                   jax.ShapeDtypeStruct((rows.ntiles * nseg, SUBLANES, D_FF), F32)],
        scratch_shapes=[
            pltpu.VMEM((tm, D_MODEL), BF16),
            pltpu.VMEM((tm + SUBLANES, FF_TILE), F32),
            pltpu.VMEM((tm, FF_TILE), F32),
            pltpu.VMEM((tm, FF_TILE), BF16),
            pltpu.VMEM((N_FF_TILES, SUBLANES, FF_TILE), F32),
        ],
        compiler_params=pltpu.CompilerParams(
            dimension_semantics=("arbitrary", "arbitrary"), vmem_limit_bytes=VMEM_LIMIT_BYTES),
        name="ffn",
    )(x1, gpre, w_up, w_up, wcf, w_down, gpost, hist_f)


def _history_block(state, nhist):
    return jnp.pad(state.astype(F32), ((0, 0), (SUBLANES - nhist, 0), (0, 0)))


def _run_group(x, lead, hist_a, hist_q, s0, hist_f, wts):
    nbatch, t, _ = x.shape
    nlead = lead.shape[0]
    tp = FRONT + nlead + t
    assert tp % CHUNK == 0, "streams must end on a chunk boundary after FRONT no-op rows"
    rows_in = _pick_rows(nbatch, tp, INPROJ_TILE_ROWS, ROW_BLOCK)
    rows = _pick_rows(nbatch, tp, MAX_TILE_ROWS, 2 * ROW_BLOCK)
    front = jnp.zeros((nbatch, FRONT, D_MODEL), F32)
    xp = jnp.concatenate([front, jnp.broadcast_to(lead[None], (nbatch, nlead, D_MODEL)), x], axis=1)
    xp = xp.reshape(rows.nrows, D_MODEL)

    p, gates, nha, nhq = _inproj(rows_in, xp, wts["gpre_mix"], wts["w_main"], wts["w_ab"], wts["wca"],
                                 wts["gna"], wts["wcg"], wts["alog"], wts["dtb"],
                                 _history_block(hist_a, 2), _history_block(hist_q, 3))
    yb, s_new = _gdn(rows, p, gates, s0.astype(F32), wts["gn_gdn"])
    x1 = _mixout(rows, xp, p, yb, wts["w_out"], wts["gpost_mix"])
    y, nhf = _ffn(rows, x1, wts["gpre_ffn"], wts["w_up"], wts["wcf"], wts["w_down"], wts["gpost_ffn"],
                  _history_block(hist_f, 2))
    y = y.reshape(nbatch, tp, D_MODEL)[:, tp - t:]
    last = lambda nh, n: nh.reshape(nbatch, -1, SUBLANES, nh.shape[-1])[None, :, -1, SUBLANES - n:]
    return (y, last(nha, 2), last(nhq, 3), s_new[None], last(nhf, 2))


def kernel(x_prompt, x_sample, state_conv_a, state_gdn_conv, state_gdn, state_ffn_conv, meta_tokens, g_pre_mix, w_in, w_conv_a, g_norm_a, w_conv_gdn, a_log, dt_bias, g_norm_gdn, w_out, g_post_mix, g_pre_ffn, w_up, w_conv_ffn, w_down, g_post_ffn):
    assert g_pre_mix.shape[0] == 1, "one layer"
    w_in0 = w_in[0]
    n_main = N_COL_TILES * COL_TILE
    lane_pad = lambda v: jnp.pad(v.astype(F32)[None], ((0, 0), (N_HEADS, LANES - 2 * N_HEADS)))
    wts = {
        "gpre_mix": g_pre_mix.astype(F32),
        "w_main": w_in0[:, :n_main].astype(BF16),
        "w_ab": jnp.pad(w_in0[:, n_main:], ((0, 0), (0, LANES - 2 * N_HEADS))).astype(BF16),
        "wca": w_conv_a[0].astype(F32),
        "gna": g_norm_a.astype(F32),
        "wcg": w_conv_gdn[0].astype(F32),
        "alog": lane_pad(a_log[0]),
        "dtb": lane_pad(dt_bias[0]),
        "gn_gdn": g_norm_gdn.astype(F32),
        "w_out": w_out[0].astype(BF16),
        "gpost_mix": g_post_mix.astype(F32),
        "gpre_ffn": g_pre_ffn.astype(F32),
        "w_up": w_up[0].astype(BF16),
        "wcf": w_conv_ffn[0].astype(F32),
        "w_down": w_down[0].astype(BF16),
        "gpost_ffn": g_post_ffn.astype(F32),
    }
    nb = x_prompt.shape[0]
    zeros = lambda *shape: jnp.zeros(shape, F32)
    yp, nca_p, ngc_p, ngd_p, nfc_p = _run_group(
        x_prompt, meta_tokens.astype(F32),
        zeros(nb, 2, CONV_WIDTH), zeros(nb, 3, QKV_WIDTH),
        zeros(nb, N_HEADS, HEAD_DIM, HEAD_DIM), zeros(nb, 2, D_FF), wts)
    ys, nca_s, ngc_s, ngd_s, nfc_s = _run_group(
        x_sample, jnp.zeros((0, D_MODEL), F32),
        state_conv_a[0], state_gdn_conv[0], state_gdn[0], state_ffn_conv[0], wts)
    return (yp, ys, nca_p, ngc_p, ngd_p, nfc_p, nca_s, ngc_s, ngd_s, nfc_s)
```

```python
import functools
from typing import NamedTuple

import jax
import jax.numpy as jnp
from jax import lax
from jax.experimental import pallas as pl
from jax.experimental.pallas import tpu as pltpu

F32 = jnp.float32
BF16 = jnp.bfloat16

D_MODEL = 2048
CHUNK = 64
N_META = 16
FRONT = CHUNK - N_META
CONV_WIDTH = 1024
N_HEADS = 8
HEAD_DIM = 128
GDN_QK = N_HEADS * HEAD_DIM
QKV_WIDTH = 3 * GDN_QK
D_FF = 5632
EPS = 1e-6

LANES = 128
SUBLANES = 8
BF16_ROWS = 16
VMEM_LIMIT_BYTES = 56 * 1024 * 1024

COL_TILE = 1024
N_COL_TILES = 7
P_TILES = 5
FF_TILE = 512
N_FF_TILES = D_FF // FF_TILE
MAX_TILE_ROWS = 576
INPROJ_TILE_ROWS = 448
ROW_BLOCK = 32


class Rows(NamedTuple):
    nbatch: int
    tp: int
    tm: int
    nseg: int
    front: int

    @property
    def tiles_per_stream(self):
        return self.tp // self.tm if self.nseg == 1 else 1

    @property
    def ntiles(self):
        return self.nbatch * self.tp // self.tm

    @property
    def nrows(self):
        return self.nbatch * self.tp


def _pick_rows(nbatch, tp, front, max_rows, multiple):
    if front and tp <= max_rows // 2:
        nseg = max(n for n in range(1, nbatch + 1) if nbatch % n == 0 and n * tp <= max_rows)
        return Rows(nbatch, tp, nseg * tp, nseg, front)
    tm = max(t for t in range(multiple, min(tp, max_rows) + 1, multiple) if tp % t == 0)
    return Rows(nbatch, tp, tm, 1, front)


def _stream_block(rows, i):
    return i // rows.tiles_per_stream if rows.nseg == 1 else i


def _is_first_tile(rows, i):
    return (i % rows.tiles_per_stream) == 0 if rows.nseg == 1 else None


def _zero_noop_rows(rows, i, val, row0=0):
    if rows.front == 0:
        return val
    r = lax.broadcasted_iota(jnp.int32, val.shape, 0) + row0
    if rows.nseg == 1:
        noop = r < jnp.where(_is_first_tile(rows, i), rows.front, 0)
    else:
        noop = (r % rows.tp) < rows.front
    return jnp.where(noop, 0.0, val)


def _rms_scale(x):
    return lax.rsqrt(jnp.mean(x * x, axis=-1, keepdims=True) + EPS)


def _silu(x):
    return x * jax.nn.sigmoid(x)


def _stage_history(cbuf, tail, hist, rows, i, nhist):
    width = cbuf.shape[1]
    if rows.front == 0:
        cbuf[0:SUBLANES, :] = jnp.where(_is_first_tile(rows, i), hist(0), tail[...])
        return
    srow = lax.broadcasted_iota(jnp.int32, (SUBLANES, width), 0)
    if rows.nseg > 1:
        cbuf[0:SUBLANES, :] = jnp.zeros((SUBLANES, width), F32)
        hist_rows = srow >= SUBLANES - nhist
    else:
        first = _is_first_tile(rows, i)
        cbuf[0:SUBLANES, :] = jnp.where(first, 0.0, tail[...])
        hist_rows = srow >= jnp.where(first, SUBLANES - nhist, SUBLANES)
    for s in range(rows.nseg):
        blk = pl.ds(s * rows.tp + rows.front, SUBLANES)
        cbuf[blk, :] = jnp.where(hist_rows, hist(s), cbuf[blk, :])


def _save_history(cbuf, tail, put, rows):
    if rows.nseg == 1:
        last = cbuf[pl.ds(rows.tm, SUBLANES), :]
        tail[...] = last
        put(0, last)
    else:
        for s in range(rows.nseg):
            put(s, cbuf[pl.ds((s + 1) * rows.tp, SUBLANES), :])


def _conv_block(cbuf, r0, nrows, cols, w):
    k = w.shape[0]
    xb = cbuf[pl.ds(r0, nrows + SUBLANES), cols]
    y = xb[SUBLANES:] * w[k - 1:k]
    for s in range(1, k):
        y = y + pltpu.roll(xb, s, axis=0)[SUBLANES:] * w[k - 1 - s:k - s]
    return y


def _row_loop(nrows, step, body, unroll=1):
    def wrapped(b, carry):
        body(pl.multiple_of(b * step, step))
        return carry
    lax.fori_loop(0, nrows // step, wrapped, 0, unroll=unroll)


NORM_UNROLL = 4


def _residual_norm(o_ref, x_ref, g_ref, nrows):
    for r0 in range(0, nrows, SUBLANES):
        blk = slice(r0, r0 + SUBLANES)
        f = o_ref[blk, :]
        o_ref[blk, :] = x_ref[blk, :] + f * _rms_scale(f) * g_ref[...]


def _inproj_kernel(x_ref, gpre_ref, w_ref, wab_ref, wca_ref, gna_ref, wcg_ref, alog_ref, dtb_ref,
                   hista_ref, histq_ref,
                   p_ref, gates_ref, nha_ref, nhq_ref,
                   h_sc, ah_sc, cb0, cb1, tail_a, tail_q, *, rows):
    i = pl.program_id(0)
    tm = rows.tm
    tile = slice(SUBLANES, SUBLANES + tm)
    row_blocks = range(0, tm, ROW_BLOCK)
    lane_blocks = [slice(c * LANES, (c + 1) * LANES) for c in range(COL_TILE // LANES)]

    @pl.when(i == 0)
    def _():
        tail_a[...] = jnp.zeros(tail_a.shape, F32)
        tail_q[...] = jnp.zeros(tail_q.shape, F32)

    for r0 in range(0, tm, BF16_ROWS):
        x = x_ref[r0:r0 + BF16_ROWS, :]
        h_sc[r0:r0 + BF16_ROWS, :] = (x * _rms_scale(x) * gpre_ref[...]).astype(BF16)

    def project(t):
        return _dot(h_sc[...], w_ref[:, t * COL_TILE:(t + 1) * COL_TILE])

    def p_cols(t, cols):
        return slice(t * COL_TILE + cols.start, t * COL_TILE + cols.stop)

    logits = _dot(h_sc[...], wab_ref[...])
    lane = lax.broadcasted_iota(jnp.int32, (tm, LANES), 1)
    g = -jnp.exp(alog_ref[...]) * jax.nn.softplus(logits + dtb_ref[...])
    val = jnp.where(lane < N_HEADS, jax.nn.sigmoid(logits), g)
    val = jnp.where(lane < 2 * N_HEADS, val, 0.0)
    gates_ref[...] = _zero_noop_rows(rows, i, val)

    ah_sc[...] = project(0)
    cb0[tile, :] = project(1)
    for rb in row_blocks:
        blk = slice(rb + SUBLANES, rb + SUBLANES + ROW_BLOCK)
        cb0[blk, :] = cb0[blk, :] * ah_sc[rb:rb + ROW_BLOCK, :]
    _stage_history(cb0, tail_a, lambda s: hista_ref[s], rows, i, 2)
    for rb in row_blocks:
        for cols in lane_blocks:
            ah_sc[rb:rb + ROW_BLOCK, cols] = _conv_block(cb0, rb, ROW_BLOCK, cols, wca_ref[:, cols])
    _save_history(cb0, tail_a, nha_ref.__setitem__, rows)

    cb1[tile, :] = project(2)
    for r0 in range(0, tm, BF16_ROWS):
        y = cb1[r0 + SUBLANES:r0 + SUBLANES + BF16_ROWS, :] * ah_sc[r0:r0 + BF16_ROWS, :]
        p_ref[r0:r0 + BF16_ROWS, 0:COL_TILE] = (y * _rms_scale(y) * gna_ref[...]).astype(BF16)

    for part, cb in ((0, cb0), (1, cb1), (2, cb0)):
        wcols = lambda cols: p_cols(part, cols)
        cb[tile, :] = project(3 + part)
        _stage_history(cb, tail_q.at[part], lambda s: histq_ref[s, :, part * COL_TILE:(part + 1) * COL_TILE],
                       rows, i, 3)
        for rb in row_blocks:
            has_noop = part == 1 and (rows.nseg > 1 or rb < rows.front)
            for cols in lane_blocks:
                s = _silu(_conv_block(cb, rb, ROW_BLOCK, cols, wcg_ref[:, wcols(cols)]))
                if part < 2:
                    s = s * lax.rsqrt(jnp.sum(s * s, axis=-1, keepdims=True) + EPS)
                if part == 0:
                    s = s * HEAD_DIM ** -0.5
                if has_noop:
                    s = _zero_noop_rows(rows, i, s, rb)
                p_ref[rb:rb + ROW_BLOCK, p_cols(1 + part, cols)] = s.astype(BF16)

        def put(s, last, part=part):
            nhq_ref[s, :, part * COL_TILE:(part + 1) * COL_TILE] = last
        _save_history(cb, tail_q.at[part], put, rows)

    cb1[tile, :] = project(6)
    for rb in row_blocks:
        z = cb1[rb + SUBLANES:rb + SUBLANES + ROW_BLOCK, :]
        p_ref[rb:rb + ROW_BLOCK, 4 * COL_TILE:5 * COL_TILE] = _silu(z).astype(BF16)


def _inproj(rows, x, gpre, w_main, w_ab, wca, gna, wcg, alog, dtb, hist_a, hist_q):
    tm, nseg = rows.tm, rows.nseg
    sb = functools.partial(_stream_block, rows)
    const = lambda i: (0, 0)
    resident = dict(pipeline_mode=pl.Buffered(1))
    return pl.pallas_call(
        functools.partial(_inproj_kernel, rows=rows),
        grid=(rows.ntiles,),
        in_specs=[
            pl.BlockSpec((tm, D_MODEL), lambda i: (i, 0)),
            pl.BlockSpec((1, D_MODEL), const),
            pl.BlockSpec((D_MODEL, N_COL_TILES * COL_TILE), const, **resident),
            pl.BlockSpec((D_MODEL, LANES), const, **resident),
            pl.BlockSpec((3, CONV_WIDTH), const),
            pl.BlockSpec((1, CONV_WIDTH), const),
            pl.BlockSpec((4, QKV_WIDTH), const),
            pl.BlockSpec((1, LANES), const),
            pl.BlockSpec((1, LANES), const),
            pl.BlockSpec((nseg, SUBLANES, CONV_WIDTH), lambda i: (sb(i), 0, 0)),
            pl.BlockSpec((nseg, SUBLANES, QKV_WIDTH), lambda i: (sb(i), 0, 0)),
        ],
        out_specs=[
            pl.BlockSpec((tm, P_TILES * COL_TILE), lambda i: (i, 0)),
            pl.BlockSpec((tm, LANES), lambda i: (i, 0)),
            pl.BlockSpec((nseg, SUBLANES, CONV_WIDTH), lambda i: (i, 0, 0)),
            pl.BlockSpec((nseg, SUBLANES, QKV_WIDTH), lambda i: (i, 0, 0)),
        ],
        out_shape=[
            jax.ShapeDtypeStruct((rows.nrows, P_TILES * COL_TILE), BF16),
            jax.ShapeDtypeStruct((rows.nrows, LANES), F32),
            jax.ShapeDtypeStruct((rows.ntiles * nseg, SUBLANES, CONV_WIDTH), F32),
            jax.ShapeDtypeStruct((rows.ntiles * nseg, SUBLANES, QKV_WIDTH), F32),
        ],
        scratch_shapes=[
            pltpu.VMEM((tm, D_MODEL), BF16),
            pltpu.VMEM((tm, COL_TILE), F32),
            pltpu.VMEM((tm + SUBLANES, COL_TILE), F32),
            pltpu.VMEM((tm + SUBLANES, COL_TILE), F32),
            pltpu.VMEM((SUBLANES, CONV_WIDTH), F32),
            pltpu.VMEM((3, SUBLANES, COL_TILE), F32),
        ],
        compiler_params=pltpu.CompilerParams(
            dimension_semantics=("arbitrary",), vmem_limit_bytes=VMEM_LIMIT_BYTES),
        name="inproj",
    )(x, gpre, w_main, w_ab, wca, gna, wcg, alog, dtb, hist_a, hist_q)


def _dot(a, b):
    return jnp.dot(a, b, preferred_element_type=F32)


def _dot_nt(a, b):
    return lax.dot_general(a, b, (((1,), (1,)), ((), ())), preferred_element_type=F32)


def _dot_tn(a, b):
    return lax.dot_general(a, b, (((0,), (0,)), ((), ())), preferred_element_type=F32)


def _gdn_kernel(q_ref, k_ref, v_ref, zs_ref, gates_ref, s0_ref, gn_ref, yb_ref, sout_ref, s_sc, *,
                nsteps, nchunks):
    n = pl.program_id(1)

    @pl.when(n == 0)
    def _():
        s_sc[...] = s0_ref[0]

    ri = lax.broadcasted_iota(jnp.int32, (CHUNK, CHUNK), 0)
    ci = lax.broadcasted_iota(jnp.int32, (CHUNK, CHUNK), 1)
    lower = ri >= ci
    strict = ri > ci
    eye = (ri == ci).astype(F32)
    tri = lower.astype(BF16)

    heads = range(N_HEADS)
    cols = [slice(h * HEAD_DIM, (h + 1) * HEAD_DIM) for h in heads]
    gl = [N_HEADS + h for h in heads]
    chunks = range(nchunks)
    chains = [(c, h) for c in chunks for h in heads]
    rws = [slice(c * CHUNK, (c + 1) * CHUNK) for c in chunks]

    gates, gates_t, gcum, gcum_t, e_cum, e_cum_t, e_rest, e_last = ([] for _ in range(8))
    for c in chunks:
        ga = gates_ref[rws[c], :]
        g_hi = ga.astype(BF16)
        rem = ga - g_hi.astype(F32)
        g_mid = rem.astype(BF16)
        g_lo = (rem - g_mid.astype(F32)).astype(BF16)
        gc = _dot(tri, g_hi) + _dot(tri, g_mid) + _dot(tri, g_lo)
        g_last = gc[CHUNK - 1:CHUNK, :]
        gates.append(ga)
        gates_t.append(ga.T)
        gcum.append(gc)
        gcum_t.append(gc.T)
        e_cum.append(jnp.exp(gc))
        e_cum_t.append(jnp.exp(gcum_t[c]))
        e_rest.append(jnp.exp(g_last - gc))
        e_last.append(jnp.exp(g_last))

    q = {(c, h): q_ref[rws[c], cols[h]] for c, h in chains}
    k = {(c, h): k_ref[rws[c], cols[h]] for c, h in chains}
    v = {(c, h): v_ref[rws[c], cols[h]] for c, h in chains}

    kq = {ch: _dot_nt(jnp.concatenate([k[ch], q[ch]], axis=0), k[ch]) for ch in chains}
    decay = {(c, h): jnp.exp(jnp.where(lower, gcum[c][:, g:g + 1] - gcum_t[c][g:g + 1, :], -jnp.inf))
             for c in chunks for h, g in zip(heads, gl)}
    m = {(c, h): gates[c][:, h:h + 1] * kq[c, h][:CHUNK] * jnp.where(strict, decay[c, h], 0.0)
         for c, h in chains}
    attn = {ch: kq[ch][CHUNK:] * decay[ch] for ch in chains}

    inv = {ch: eye - m[ch] for ch in chains}
    pw = {ch: m[ch].astype(BF16) for ch in chains}
    for _ in range(5):
        pw = {ch: _dot(pw[ch], pw[ch]).astype(BF16) for ch in chains}
        inv = {ch: inv[ch] + _dot(inv[ch].astype(BF16), pw[ch]) for ch in chains}

    inv_b = {(c, h): inv[c, h] * gates_t[c][h:h + 1, :] for c, h in chains}
    u = {ch: _dot(inv_b[ch].astype(BF16), v[ch]) for ch in chains}
    w = {(c, h): _dot((inv_b[c, h] * e_cum_t[c][g:g + 1, :]).astype(BF16), k[c, h])
         for c in chunks for h, g in zip(heads, gl)}

    s = [s_sc[h] for h in heads]
    for c in chunks:
        s_b = [x.astype(BF16) for x in s]
        ws = [_dot(w[c, h].astype(BF16), s_b[h]) for h in heads]
        qs = [_dot(q[c, h], s_b[h]) for h in heads]
        v_new = [u[c, h] - ws[h] for h in heads]
        av = [_dot(attn[c, h].astype(BF16), v_new[h].astype(BF16)) for h in heads]
        kv = [_dot_tn(k[c, h], (v_new[h] * e_rest[c][:, g:g + 1]).astype(BF16)) for h, g in zip(heads, gl)]
        s = [s[h] * e_last[c][:, g:g + 1] + kv[h] for h, g in zip(heads, gl)]
        for h, g in zip(heads, gl):
            o = e_cum[c][:, g:g + 1] * qs[h] + av[h]
            yb = o * _rms_scale(o) * gn_ref[...] * zs_ref[rws[c], cols[h]].astype(F32)
            yb_ref[rws[c], cols[h]] = yb.astype(BF16)
    for h in heads:
        s_sc[h] = s[h]

    @pl.when(n == nsteps - 1)
    def _():
        sout_ref[0] = s_sc[...]


GDN_MAX_CHUNKS_PER_STEP = 5


def _gdn(rows, p, gates, s0, gn):
    nc = rows.tp // CHUNK
    cps = max(c for c in range(1, GDN_MAX_CHUNKS_PER_STEP + 1) if nc % c == 0)
    nsteps = nc // cps
    blk = lambda t: pl.BlockSpec((cps * CHUNK, COL_TILE), lambda b, n: (b * nsteps + n, t))
    state = pl.BlockSpec((1, N_HEADS, HEAD_DIM, HEAD_DIM), lambda b, n: (b, 0, 0, 0))
    return pl.pallas_call(
        functools.partial(_gdn_kernel, nsteps=nsteps, nchunks=cps),
        grid=(rows.nbatch, nsteps),
        in_specs=[blk(1), blk(2), blk(3), blk(4),
                  pl.BlockSpec((cps * CHUNK, LANES), lambda b, n: (b * nsteps + n, 0)),
                  state,
                  pl.BlockSpec((1, HEAD_DIM), lambda b, n: (0, 0))],
        out_specs=[pl.BlockSpec((cps * CHUNK, GDN_QK), lambda b, n: (b * nsteps + n, 0)), state],
        out_shape=[jax.ShapeDtypeStruct((rows.nrows, GDN_QK), BF16),
                   jax.ShapeDtypeStruct((rows.nbatch, N_HEADS, HEAD_DIM, HEAD_DIM), F32)],
        scratch_shapes=[pltpu.VMEM((N_HEADS, HEAD_DIM, HEAD_DIM), F32)],
        compiler_params=pltpu.CompilerParams(
            dimension_semantics=("arbitrary", "arbitrary"), vmem_limit_bytes=VMEM_LIMIT_BYTES),
        name="gdn",
    )(p, p, p, p, gates, s0, gn)


def _mixout_kernel(x_ref, ya_ref, yb_ref, w_ref, g_ref, o_ref, *, rows):
    o_ref[...] = (_dot(ya_ref[...], w_ref[0:CONV_WIDTH, :])
                  + _dot(yb_ref[...], w_ref[CONV_WIDTH:CONV_WIDTH + GDN_QK, :]))

    _residual_norm(o_ref, x_ref, g_ref, rows.tm)


def _mixout(rows, x, p, yb, w_out, gpost):
    tm = rows.tm
    row_tile = lambda width: pl.BlockSpec((tm, width), lambda i: (i, 0))
    return pl.pallas_call(
        functools.partial(_mixout_kernel, rows=rows),
        grid=(rows.ntiles,),
        in_specs=[row_tile(D_MODEL), row_tile(COL_TILE), row_tile(GDN_QK),
                  pl.BlockSpec((CONV_WIDTH + GDN_QK, D_MODEL), lambda i: (0, 0)),
                  pl.BlockSpec((1, D_MODEL), lambda i: (0, 0))],
        out_specs=row_tile(D_MODEL),
        out_shape=jax.ShapeDtypeStruct((rows.nrows, D_MODEL), F32),
        compiler_params=pltpu.CompilerParams(
            dimension_semantics=("arbitrary",), vmem_limit_bytes=VMEM_LIMIT_BYTES),
        name="mixout",
    )(x, p, yb, w_out, gpost)


def _ffn_kernel(x_ref, gpre_ref, wg_ref, wv_ref, wc_ref, wd_ref, gpost_ref, hist_ref,
                o_ref, nh_ref, h_sc, cbuf, uv_sc, act_sc, tail, *, rows):
    i = pl.program_id(0)
    c = pl.program_id(1)
    tm = rows.tm

    half = tm // 2

    @pl.when(c == 0)
    def _():
        @pl.when(i == 0)
        def _():
            tail[...] = jnp.zeros(tail.shape, F32)

        def norm(r0):
            blk = pl.ds(r0, BF16_ROWS)
            x = x_ref[blk, :]
            h_sc[blk, :] = (x * _rms_scale(x) * gpre_ref[...]).astype(BF16)
            o_ref[blk, :] = jnp.zeros((BF16_ROWS, D_MODEL), F32)
        _row_loop(tm, BF16_ROWS, norm, NORM_UNROLL)

    for r0 in (0, half):
        h = h_sc[r0:r0 + half, :]
        cbuf[SUBLANES + r0:SUBLANES + r0 + half, :] = _dot(h, wg_ref[...])
        uv_sc[r0:r0 + half, :] = _dot(h, wv_ref[...])
    _stage_history(cbuf, tail.at[c], lambda s: hist_ref[s], rows, i, 2)
    for r0 in (0, half):
        for rb in range(r0, r0 + half, ROW_BLOCK):
            blk = slice(rb, rb + ROW_BLOCK)
            for t in range(FF_TILE // LANES):
                cols = slice(t * LANES, (t + 1) * LANES)
                gate = _silu(_conv_block(cbuf, rb, ROW_BLOCK, cols, wc_ref[:, cols]))
                act_sc[blk, cols] = (gate * uv_sc[blk, cols]).astype(BF16)
        o_ref[r0:r0 + half, :] += _dot(act_sc[r0:r0 + half, :], wd_ref[...])
    _save_history(cbuf, tail.at[c], nh_ref.__setitem__, rows)

    @pl.when(c == pl.num_programs(1) - 1)
    def _():
        _residual_norm(o_ref, x_ref, gpost_ref, tm)


def _ffn(rows, x1, gpre, w_up, wcf, w_down, gpost, hist_f):
    tm, nseg = rows.tm, rows.nseg
    sb = functools.partial(_stream_block, rows)
    const = lambda i, c: (0, 0)
    hist = pl.BlockSpec((nseg, SUBLANES, FF_TILE), lambda i, c: (sb(i), 0, c))
    return pl.pallas_call(
        functools.partial(_ffn_kernel, rows=rows),
        grid=(rows.ntiles, N_FF_TILES),
        in_specs=[
            pl.BlockSpec((tm, D_MODEL), lambda i, c: (i, 0)),
            pl.BlockSpec((1, D_MODEL), const),
            pl.BlockSpec((D_MODEL, FF_TILE), lambda i, c: (0, c)),
            pl.BlockSpec((D_MODEL, FF_TILE), lambda i, c: (0, N_FF_TILES + c)),
            pl.BlockSpec((3, FF_TILE), lambda i, c: (0, c)),
            pl.BlockSpec((FF_TILE, D_MODEL), lambda i, c: (c, 0)),
            pl.BlockSpec((1, D_MODEL), const),
            hist,
        ],
        out_specs=[pl.BlockSpec((tm, D_MODEL), lambda i, c: (i, 0)),
                   pl.BlockSpec((nseg, SUBLANES, FF_TILE), lambda i, c: (i, 0, c))],
        out_shape=[jax.ShapeDtypeStruct((rows.nrows, D_MODEL), F32),
                   jax.ShapeDtypeStruct((rows.ntiles * nseg, SUBLANES, D_FF), F32)],
        scratch_shapes=[
            pltpu.VMEM((tm, D_MODEL), BF16),
            pltpu.VMEM((tm + SUBLANES, FF_TILE), F32),
            pltpu.VMEM((tm, FF_TILE), F32),
            pltpu.VMEM((tm, FF_TILE), BF16),
            pltpu.VMEM((N_FF_TILES, SUBLANES, FF_TILE), F32),
        ],
        compiler_params=pltpu.CompilerParams(
            dimension_semantics=("arbitrary", "arbitrary"), vmem_limit_bytes=VMEM_LIMIT_BYTES),
        name="ffn",
    )(x1, gpre, w_up, w_up, wcf, w_down, gpost, hist_f)


def _history_block(state, nhist):
    return jnp.pad(state.astype(F32), ((0, 0), (SUBLANES - nhist, 0), (0, 0)))


def _run_group(x, hist_a, hist_q, s0, hist_f, wts):
    nbatch, t, _ = x.shape
    front = -t % CHUNK
    assert front in (0, FRONT), "streams are whole chunks or N_META-token tails"
    tp = front + t
    rows_in = _pick_rows(nbatch, tp, front, INPROJ_TILE_ROWS, ROW_BLOCK)
    rows = _pick_rows(nbatch, tp, front, MAX_TILE_ROWS, 2 * ROW_BLOCK)
    xp = jnp.pad(x.astype(F32), ((0, 0), (front, 0), (0, 0))).reshape(rows.nrows, D_MODEL)

    p, gates, nha, nhq = _inproj(rows_in, xp, wts["gpre_mix"], wts["w_main"], wts["w_ab"], wts["wca"],
                                 wts["gna"], wts["wcg"], wts["alog"], wts["dtb"],
                                 _history_block(hist_a, 2), _history_block(hist_q, 3))
    yb, s_new = _gdn(rows, p, gates, s0.astype(F32), wts["gn_gdn"])
    x1 = _mixout(rows, xp, p, yb, wts["w_out"], wts["gpost_mix"])
    y, nhf = _ffn(rows, x1, wts["gpre_ffn"], wts["w_up"], wts["wcf"], wts["w_down"], wts["gpost_ffn"],
                  _history_block(hist_f, 2))
    y = y.reshape(nbatch, tp, D_MODEL)[:, front:]
    last = lambda nh, n: nh.reshape(nbatch, -1, SUBLANES, nh.shape[-1])[:, -1, SUBLANES - n:]
    return y, last(nha, 2), last(nhq, 3), s_new, last(nhf, 2)


def kernel(x_prompt, x_sample, state_conv_a, state_gdn_conv, state_gdn, state_ffn_conv, meta_tokens, g_pre_mix, w_in, w_conv_a, g_norm_a, w_conv_gdn, a_log, dt_bias, g_norm_gdn, w_out, g_post_mix, g_pre_ffn, w_up, w_conv_ffn, w_down, g_post_ffn):
    assert g_pre_mix.shape[0] == 1, "one layer"
    w_in0 = w_in[0]
    n_main = N_COL_TILES * COL_TILE
    lane_pad = lambda v: jnp.pad(v.astype(F32)[None], ((0, 0), (N_HEADS, LANES - 2 * N_HEADS)))
    wts = {
        "gpre_mix": g_pre_mix.astype(F32),
        "w_main": w_in0[:, :n_main].astype(BF16),
        "w_ab": jnp.pad(w_in0[:, n_main:], ((0, 0), (0, LANES - 2 * N_HEADS))).astype(BF16),
        "wca": w_conv_a[0].astype(F32),
        "gna": g_norm_a.astype(F32),
        "wcg": w_conv_gdn[0].astype(F32),
        "alog": lane_pad(a_log[0]),
        "dtb": lane_pad(dt_bias[0]),
        "gn_gdn": g_norm_gdn.astype(F32),
        "w_out": w_out[0].astype(BF16),
        "gpost_mix": g_post_mix.astype(F32),
        "gpre_ffn": g_pre_ffn.astype(F32),
        "w_up": w_up[0].astype(BF16),
        "wcf": w_conv_ffn[0].astype(F32),
        "w_down": w_down[0].astype(BF16),
        "gpost_ffn": g_post_ffn.astype(F32),
    }
    nb, ns = x_prompt.shape[0], x_sample.shape[0]
    assert meta_tokens.shape[0] == x_sample.shape[1], "meta stream is batched with the sample streams"
    with_meta = lambda st: jnp.concatenate([st.astype(F32), jnp.zeros((1,) + st.shape[1:], F32)])
    short = _run_group(
        jnp.concatenate([x_sample.astype(F32), meta_tokens.astype(F32)[None]]),
        with_meta(state_conv_a[0]), with_meta(state_gdn_conv[0]), with_meta(state_gdn[0]),
        with_meta(state_ffn_conv[0]), wts)
    ys, nca_s, ngc_s, ngd_s, nfc_s = (o[:ns] for o in short)
    after_meta = (jnp.broadcast_to(o[ns:], (nb,) + o.shape[1:]) for o in short[1:])
    yp, nca_p, ngc_p, ngd_p, nfc_p = _run_group(x_prompt, *after_meta, wts)
    return (yp, ys, nca_p[None], ngc_p[None], ngd_p[None], nfc_p[None],
            nca_s[None], ngc_s[None], ngd_s[None], nfc_s[None])
```
